```python
import math
import jax, jax.numpy as jnp
from jax import lax
import numpy as np

D_MODEL = 2048
BATCH = 2
SEQ = 4096
DEPTH = 4
DEC_BATCH = 8
DEC_SEQ = 8
PAST_LEN = 16384
PAGE_SIZE = 128

MEM_LEN = 256
POOL_WIDTH = D_MODEL // 4
POOL_WINDOWS = (2, 4, 8, 16)
N_POOL_GROUPS = len(POOL_WINDOWS)
POOL_GROUP = POOL_WIDTH // N_POOL_GROUPS
POOL_BUF = max(POOL_WINDOWS) - 1
SSM_WIDTH = D_MODEL // 4
SSM_GROUP = 16
N_SSM_GROUPS = SSM_WIDTH // SSM_GROUP
SSM_STATE = 64
HEAD_DIM = 128
ATTN_WIDTH = D_MODEL // 2
N_HEADS = ATTN_WIDTH // HEAD_DIM
MOBA_BLOCK = 256
MOBA_TOPK = 3
Q_CHUNK = 64
N_MEM_HEADS = 4
MEM_HEAD_DIM = 128
MEM_WIDTH = N_MEM_HEADS * MEM_HEAD_DIM
N_BRANCH = 3
SPLITS = (POOL_WIDTH, POOL_WIDTH + SSM_WIDTH, POOL_WIDTH + SSM_WIDTH + ATTN_WIDTH,
          POOL_WIDTH + SSM_WIDTH + 2 * ATTN_WIDTH, POOL_WIDTH + SSM_WIDTH + 3 * ATTN_WIDTH)
IN_WIDTH = SPLITS[-1] + N_BRANCH * D_MODEL
D_FF = 256 * ((8 * D_MODEL // 3 + 255) // 256)
ALPHA = (2 * DEPTH) ** 0.25
BETA = (8 * DEPTH) ** -0.25
LN_EPS = 1e-5
NEG_INF = -1e30

kernel_name = 'moba_pool_s5_gated_hybrid_step'


def layer_norm(x, g, b):
    xf = x.astype(jnp.float32)
    mu = xf.mean(-1, keepdims=True)
    var = jnp.square(xf - mu).mean(-1, keepdims=True)
    return ((xf - mu) * lax.rsqrt(var + LN_EPS) * g.astype(jnp.float32) + b.astype(jnp.float32)).astype(x.dtype)


def swiglu(x, w_gate, w_up, w_down):
    return (jax.nn.silu(x @ w_gate) * (x @ w_up)) @ w_down


def alibi_slopes(n_heads):
    return jnp.exp2(-8.0 * jnp.arange(1, n_heads + 1, dtype=jnp.float32) / n_heads)


def pool_mixer(u, buf, pos, w_grp, scale):
    b, t, _ = u.shape
    xx = jnp.concatenate([buf.astype(u.dtype), u], axis=1).astype(jnp.float32)
    cs = jnp.pad(jnp.cumsum(xx, axis=1), ((0, 0), (1, 0), (0, 0)))
    end = cs[:, POOL_BUF + 1:POOL_BUF + 1 + t]
    outs = []
    for gi, w in enumerate(POOL_WINDOWS):
        sl = slice(gi * POOL_GROUP, (gi + 1) * POOL_GROUP)
        win_sum = end[..., sl] - cs[:, POOL_BUF + 1 - w:POOL_BUF + 1 - w + t, sl]
        cnt = jnp.minimum(pos + 1, w).astype(jnp.float32)[None, :, None]
        outs.append(win_sum / cnt)
    pooled = jnp.concatenate(outs, axis=-1) - u.astype(jnp.float32)
    pooled = pooled.reshape(b, t, N_POOL_GROUPS, POOL_GROUP)
    mixed = jnp.einsum('btgc,gcd->btgd', pooled, w_grp.astype(jnp.float32)).reshape(b, t, POOL_WIDTH)
    mixed = mixed * scale.astype(jnp.float32)
    new_buf = xx[:, -POOL_BUF:]
    return mixed.astype(u.dtype), new_buf.astype(buf.dtype)


def _cplx_combine(e1, e2):
    a1r, a1i, b1r, b1i = e1
    a2r, a2i, b2r, b2i = e2
    return (a1r * a2r - a1i * a2i, a1r * a2i + a1i * a2r,
            a2r * b1r - a2i * b1i + b2r, a2r * b1i + a2i * b1r + b2i)


def s5_mixer(u, s0_re, s0_im, lam_re, lam_im, log_dt, b_re, b_im, c_re, c_im, d_skip, w_glu, b_glu):
    f32 = jnp.float32
    b, t, _ = u.shape
    uf = u.astype(f32).reshape(b, t, N_SSM_GROUPS, SSM_GROUP)
    dt = jnp.exp(log_dt.astype(f32))[:, None]
    lr, li = lam_re.astype(f32), lam_im.astype(f32)
    mag = jnp.exp(lr * dt)
    a_re, a_im = mag * jnp.cos(li * dt), mag * jnp.sin(li * dt)
    den = lr * lr + li * li
    z_re = ((a_re - 1.0) * lr + a_im * li) / den
    z_im = (a_im * lr - (a_re - 1.0) * li) / den
    br, bi = b_re.astype(f32), b_im.astype(f32)
    bt_re = z_re[..., None] * br - z_im[..., None] * bi
    bt_im = z_re[..., None] * bi + z_im[..., None] * br
    x_re = jnp.einsum('btgh,gph->btgp', uf, bt_re)
    x_im = jnp.einsum('btgh,gph->btgp', uf, bt_im)
    s0r, s0i = s0_re.astype(f32), s0_im.astype(f32)
    x_re = x_re.at[:, 0].add(a_re * s0r - a_im * s0i)
    x_im = x_im.at[:, 0].add(a_re * s0i + a_im * s0r)
    ar = jnp.broadcast_to(a_re, x_re.shape)
    ai = jnp.broadcast_to(a_im, x_im.shape)
    _, _, s_re, s_im = lax.associative_scan(_cplx_combine, (ar, ai, x_re, x_im), axis=1)
    y = (jnp.einsum('ghp,btgp->btgh', c_re.astype(f32), s_re)
         - jnp.einsum('ghp,btgp->btgh', c_im.astype(f32), s_im))
    y = y.reshape(b, t, SSM_WIDTH) + d_skip.astype(f32) * u.astype(f32)
    y = jax.nn.gelu(y)
    y = y * jax.nn.sigmoid(y @ w_glu.astype(f32) + b_glu.astype(f32))
    return y.astype(u.dtype), s_re[:, -1].astype(s0_re.dtype), s_im[:, -1].astype(s0_im.dtype)


def moba_attention(q, k_all, v_all, q_pos):
    b, t = q.shape[:2]
    seq_len = k_all.shape[1]
    n_blk = -(-seq_len // MOBA_BLOCK)
    pad = n_blk * MOBA_BLOCK - seq_len

    def blocks(a):
        a = jnp.pad(a, ((0, 0), (0, pad), (0, 0), (0, 0)))
        return a.reshape(b, n_blk, MOBA_BLOCK, N_HEADS, HEAD_DIM).transpose(0, 3, 1, 2, 4)

    kb, vb = blocks(k_all), blocks(v_all)
    k_mean = kb.astype(jnp.float32).mean(axis=3)
    own = q_pos // MOBA_BLOCK
    gate = jnp.einsum('bthd,bhnd->bhtn', q.astype(jnp.float32), k_mean)
    fully_past = jnp.arange(n_blk)[None, :] < own[:, None]
    gate = jnp.where(fully_past, gate, NEG_INF)
    n_sel = min(MOBA_TOPK, n_blk)
    _, top_idx = lax.top_k(gate, n_sel)
    top_ok = top_idx < own[:, None]
    own_b = jnp.broadcast_to(own[:, None], (b, N_HEADS, t, 1)).astype(jnp.int32)
    sel = jnp.concatenate([top_idx.astype(jnp.int32), own_b], axis=-1)
    sel_ok = jnp.concatenate([top_ok, jnp.ones(own_b.shape, dtype=bool)], axis=-1)

    qc = math.gcd(t, Q_CHUNK)
    nc = t // qc

    def to_chunks(a):
        a = a.reshape(a.shape[:2] + (nc, qc) + a.shape[3:])
        return jnp.moveaxis(a, 2, 0)

    q_c = to_chunks(q.transpose(0, 2, 1, 3))
    sel_c, ok_c = to_chunks(sel), to_chunks(sel_ok)
    pos_c = q_pos.reshape(nc, qc)
    slopes = alibi_slopes(N_HEADS)[None, :, None, None, None]
    b_ix = jnp.arange(b)[:, None, None, None]
    h_ix = jnp.arange(N_HEADS)[None, :, None, None]
    offs = jnp.arange(MOBA_BLOCK, dtype=jnp.int32)
    scale = HEAD_DIM ** -0.5

    def attend(args):
        qh, blk, ok, pos = args
        kg = kb[b_ix, h_ix, blk]
        vg = vb[b_ix, h_ix, blk]
        s = jnp.einsum('bhqd,bhqjsd->bhqjs', qh, kg).astype(jnp.float32) * scale
        kpos = blk[..., None] * MOBA_BLOCK + offs
        dist = pos[:, None, None] - kpos
        allowed = ok[..., None] & (dist >= 0)
        s = jnp.where(allowed, s - slopes * dist.astype(jnp.float32), NEG_INF)
        shp = s.shape
        p = jax.nn.softmax(s.reshape(shp[:3] + (-1,)), axis=-1).reshape(shp)
        return jnp.einsum('bhqjs,bhqjsd->bhqd', p.astype(vg.dtype), vg)

    out = lax.map(attend, (q_c, sel_c, ok_c, pos_c))
    out = jnp.moveaxis(out, 0, 2).reshape(b, N_HEADS, t, HEAD_DIM)
    return out.transpose(0, 2, 1, 3)


def token_mixing(h, pool_buf, ssm_re, ssm_im, k_past, v_past, start, lp):
    b, t, _ = h.shape
    z = h @ lp['w_in']
    u_pool, u_ssm, q, k, v, g = jnp.split(z, SPLITS, axis=-1)
    pos = start + jnp.arange(t, dtype=jnp.int32)
    a_out, new_pool = pool_mixer(u_pool, pool_buf, pos, lp['pool_w'], lp['pool_scale'])
    b_out, new_re, new_im = s5_mixer(u_ssm, ssm_re, ssm_im, lp['ssm_lam_re'], lp['ssm_lam_im'],
                                     lp['ssm_log_dt'], lp['ssm_b_re'], lp['ssm_b_im'], lp['ssm_c_re'],
                                     lp['ssm_c_im'], lp['ssm_d'], lp['ssm_glu_w'], lp['ssm_glu_b'])
    q = q.reshape(b, t, N_HEADS, HEAD_DIM)
    k = k.reshape(b, t, N_HEADS, HEAD_DIM)
    v = v.reshape(b, t, N_HEADS, HEAD_DIM)
    k_all = k if k_past is None else jnp.concatenate([k_past.astype(k.dtype), k], axis=1)
    v_all = v if v_past is None else jnp.concatenate([v_past.astype(v.dtype), v], axis=1)
    c_out = moba_attention(q, k_all, v_all, pos).reshape(b, t, ATTN_WIDTH)
    gates = jax.nn.sigmoid((g + lp['b_gate']).astype(jnp.float32)).astype(h.dtype)
    gates = gates.reshape(b, t, N_BRANCH, D_MODEL)
    merged = (gates[:, :, 0] * (a_out @ lp['w_proj_pool'])
              + gates[:, :, 1] * (b_out @ lp['w_proj_ssm'])
              + gates[:, :, 2] * (c_out @ lp['w_proj_attn']))
    return merged @ lp['w_out'], new_pool, new_re, new_im, k, v


def memory_kv(mem, w_k, w_v):
    b, m, _ = mem.shape
    return ((mem @ w_k).reshape(b, m, N_MEM_HEADS, MEM_HEAD_DIM),
            (mem @ w_v).reshape(b, m, N_MEM_HEADS, MEM_HEAD_DIM))


def memory_attention(h, mem_k, mem_v, w_q, w_o):
    b, t, _ = h.shape
    q = (h @ w_q).reshape(b, t, N_MEM_HEADS, MEM_HEAD_DIM)
    s = jnp.einsum('bthd,bmhd->bhtm', q, mem_k.astype(q.dtype)).astype(jnp.float32) * (MEM_HEAD_DIM ** -0.5)
    p = jax.nn.softmax(s, axis=-1).astype(h.dtype)
    o = jnp.einsum('bhtm,bmhd->bthd', p, mem_v.astype(h.dtype)).reshape(b, t, MEM_WIDTH)
    return o @ w_o


def decoder_layer(h, pool_buf, ssm_re, ssm_im, k_past, v_past, mem_k, mem_v, start, lp):
    g, bb = lp['ln_g'], lp['ln_b']
    h = layer_norm(ALPHA * h + 0.5 * swiglu(h, lp['ffn1_gate'], lp['ffn1_up'], lp['ffn1_down']), g[0], bb[0])
    mix, new_pool, new_re, new_im, k_new, v_new = token_mixing(h, pool_buf, ssm_re, ssm_im, k_past, v_past, start, lp)
    h = layer_norm(ALPHA * h + mix, g[1], bb[1])
    h = layer_norm(ALPHA * h + memory_attention(h, mem_k, mem_v, lp['mem_wq'], lp['mem_wo']), g[2], bb[2])
    h = layer_norm(ALPHA * h + 0.5 * swiglu(h, lp['ffn2_gate'], lp['ffn2_up'], lp['ffn2_down']), g[3], bb[3])
    return h, new_pool, new_re, new_im, k_new, v_new


def setup_inputs(seed: int = 0) -> dict:
    key = jax.random.key(seed)
    keys = list(jax.random.split(key, 64))

    def nrm(shape, scale):
        return jax.random.normal(keys.pop(), shape, jnp.float32) * scale

    n_pages = PAST_LEN // PAGE_SIZE
    n_used = DEC_BATCH * n_pages
    n_phys = n_used + max(1, n_used // 4)
    L, d, f = DEPTH, D_MODEL, D_FF
    lam_n = jnp.arange(SSM_STATE, dtype=jnp.float32)
    inp = {}
    inp['x_prompt'] = nrm((BATCH, SEQ, d), 1.0)
    inp['x_sample'] = nrm((DEC_BATCH, DEC_SEQ, d), 1.0)
    inp['cache_k'] = nrm((L, n_phys, PAGE_SIZE, N_HEADS, HEAD_DIM), 1.0)
    inp['cache_v'] = nrm((L, n_phys, PAGE_SIZE, N_HEADS, HEAD_DIM), 1.0)
    inp['cache_mem_k'] = nrm((L, DEC_BATCH, MEM_LEN, N_MEM_HEADS, MEM_HEAD_DIM), 1.0)
    inp['cache_mem_v'] = nrm((L, DEC_BATCH, MEM_LEN, N_MEM_HEADS, MEM_HEAD_DIM), 1.0)
    inp['state_pool'] = nrm((L, DEC_BATCH, POOL_BUF, POOL_WIDTH), 1.0)
    inp['state_ssm_re'] = nrm((L, DEC_BATCH, N_SSM_GROUPS, SSM_STATE), 0.3)
    inp['state_ssm_im'] = nrm((L, DEC_BATCH, N_SSM_GROUPS, SSM_STATE), 0.3)
    inp['page_table'] = jax.random.permutation(keys.pop(), n_phys)[:n_used].reshape(DEC_BATCH, n_pages).astype(jnp.int32)
    inp['mem_prompt'] = nrm((BATCH, MEM_LEN, d), 1.0)
    inp['ln_g'] = 1.0 + nrm((L, 4, d), 0.02)
    inp['ln_b'] = nrm((L, 4, d), 0.02)
    inp['ffn1_gate'] = nrm((L, d, f), d ** -0.5)
    inp['ffn1_up'] = nrm((L, d, f), d ** -0.5)
    inp['ffn1_down'] = nrm((L, f, d), BETA * f ** -0.5)
    inp['w_in'] = nrm((L, d, IN_WIDTH), d ** -0.5)
    inp['b_gate'] = nrm((L, N_BRANCH * d), 0.02)
    inp['pool_w'] = nrm((L, N_POOL_GROUPS, POOL_GROUP, POOL_GROUP), POOL_GROUP ** -0.5)
    inp['pool_scale'] = 1.0 + nrm((L, POOL_WIDTH), 0.02)
    inp['ssm_lam_re'] = -0.5 * jnp.exp(nrm((L, N_SSM_GROUPS, SSM_STATE), 0.01))
    inp['ssm_lam_im'] = math.pi * lam_n + nrm((L, N_SSM_GROUPS, SSM_STATE), 0.01)
    inp['ssm_log_dt'] = jax.random.uniform(keys.pop(), (L, N_SSM_GROUPS), jnp.float32, math.log(1e-3), math.log(1e-1))
    inp['ssm_b_re'] = nrm((L, N_SSM_GROUPS, SSM_STATE, SSM_GROUP), SSM_GROUP ** -0.5)
    inp['ssm_b_im'] = nrm((L, N_SSM_GROUPS, SSM_STATE, SSM_GROUP), SSM_GROUP ** -0.5)
    inp['ssm_c_re'] = nrm((L, N_SSM_GROUPS, SSM_GROUP, SSM_STATE), SSM_STATE ** -0.5)
    inp['ssm_c_im'] = nrm((L, N_SSM_GROUPS, SSM_GROUP, SSM_STATE), SSM_STATE ** -0.5)
    inp['ssm_d'] = nrm((L, SSM_WIDTH), 1.0)
    inp['ssm_glu_w'] = nrm((L, SSM_WIDTH, SSM_WIDTH), SSM_WIDTH ** -0.5)
    inp['ssm_glu_b'] = nrm((L, SSM_WIDTH), 0.02)
    inp['w_proj_pool'] = nrm((L, POOL_WIDTH, d), POOL_WIDTH ** -0.5)
    inp['w_proj_ssm'] = nrm((L, SSM_WIDTH, d), SSM_WIDTH ** -0.5)
    inp['w_proj_attn'] = nrm((L, ATTN_WIDTH, d), ATTN_WIDTH ** -0.5)
    inp['w_out'] = nrm((L, d, d), BETA * d ** -0.5)
    inp['mem_wq'] = nrm((L, d, MEM_WIDTH), d ** -0.5)
    inp['mem_wk'] = nrm((L, d, MEM_WIDTH), d ** -0.5)
    inp['mem_wv'] = nrm((L, d, MEM_WIDTH), d ** -0.5)
    inp['mem_wo'] = nrm((L, MEM_WIDTH, d), BETA * MEM_WIDTH ** -0.5)
    inp['ffn2_gate'] = nrm((L, d, f), d ** -0.5)
    inp['ffn2_up'] = nrm((L, d, f), d ** -0.5)
    inp['ffn2_down'] = nrm((L, f, d), BETA * f ** -0.5)
    return inp


def reference(x_prompt, x_sample, cache_k, cache_v, cache_mem_k, cache_mem_v, state_pool,
              state_ssm_re, state_ssm_im, page_table, mem_prompt, ln_g, ln_b, ffn1_gate, ffn1_up,
              ffn1_down, w_in, b_gate, pool_w, pool_scale, ssm_lam_re, ssm_lam_im, ssm_log_dt,
              ssm_b_re, ssm_b_im, ssm_c_re, ssm_c_im, ssm_d, ssm_glu_w, ssm_glu_b, w_proj_pool,
              w_proj_ssm, w_proj_attn, w_out, mem_wq, mem_wk, mem_wv, mem_wo, ffn2_gate, ffn2_up,
              ffn2_down):
    n_pages = PAST_LEN // PAGE_SIZE
    past = n_pages * PAGE_SIZE
    dec_b = page_table.shape[0]
    hp, hs = x_prompt, x_sample
    kp_l, vp_l, poolp_l, rep_l, imp_l, mkp_l, mvp_l = [], [], [], [], [], [], []
    ks_l, vs_l, pools_l, res_l, ims_l = [], [], [], [], []
    for l in range(DEPTH):
        lp = {'ln_g': ln_g[l], 'ln_b': ln_b[l],
              'ffn1_gate': ffn1_gate[l], 'ffn1_up': ffn1_up[l], 'ffn1_down': ffn1_down[l],
              'w_in': w_in[l], 'b_gate': b_gate[l], 'pool_w': pool_w[l], 'pool_scale': pool_scale[l],
              'ssm_lam_re': ssm_lam_re[l], 'ssm_lam_im': ssm_lam_im[l], 'ssm_log_dt': ssm_log_dt[l],
              'ssm_b_re': ssm_b_re[l], 'ssm_b_im': ssm_b_im[l], 'ssm_c_re': ssm_c_re[l],
              'ssm_c_im': ssm_c_im[l], 'ssm_d': ssm_d[l], 'ssm_glu_w': ssm_glu_w[l],
              'ssm_glu_b': ssm_glu_b[l], 'w_proj_pool': w_proj_pool[l], 'w_proj_ssm': w_proj_ssm[l],
              'w_proj_attn': w_proj_attn[l], 'w_out': w_out[l], 'mem_wq': mem_wq[l], 'mem_wo': mem_wo[l],
              'ffn2_gate': ffn2_gate[l], 'ffn2_up': ffn2_up[l], 'ffn2_down': ffn2_down[l]}
        mk_p, mv_p = memory_kv(mem_prompt, mem_wk[l], mem_wv[l])
        bp = hp.shape[0]
        z_pool = jnp.zeros((bp, POOL_BUF, POOL_WIDTH), hp.dtype)
        z_ssm = jnp.zeros((bp, N_SSM_GROUPS, SSM_STATE), hp.dtype)
        hp, pool_p, re_p, im_p, k_p, v_p = decoder_layer(hp, z_pool, z_ssm, z_ssm, None, None,
                                                         mk_p, mv_p, 0, lp)
        k_past = cache_k[l][page_table].reshape(dec_b, past, N_HEADS, HEAD_DIM)
        v_past = cache_v[l][page_table].reshape(dec_b, past, N_HEADS, HEAD_DIM)
        hs, pool_s, re_s, im_s, k_s, v_s = decoder_layer(hs, state_pool[l], state_ssm_re[l], state_ssm_im[l],
                                                         k_past, v_past, cache_mem_k[l], cache_mem_v[l],
                                                         PAST_LEN, lp)
        kp_l.append(k_p); vp_l.append(v_p); poolp_l.append(pool_p); rep_l.append(re_p); imp_l.append(im_p)
        mkp_l.append(mk_p); mvp_l.append(mv_p)
        ks_l.append(k_s); vs_l.append(v_s); pools_l.append(pool_s); res_l.append(re_s); ims_l.append(im_s)
    return (hp, hs,
            jnp.stack(kp_l), jnp.stack(vp_l), jnp.stack(poolp_l), jnp.stack(rep_l), jnp.stack(imp_l),
            jnp.stack(mkp_l), jnp.stack(mvp_l),
            jnp.stack(ks_l), jnp.stack(vs_l), jnp.stack(pools_l), jnp.stack(res_l), jnp.stack(ims_l))
```

```python
import functools
import math

import jax
import jax.numpy as jnp
from jax import lax
from jax.experimental import pallas as pl
from jax.experimental.pallas import tpu as pltpu

F32 = jnp.float32
BF16 = jnp.bfloat16

POOL_WINDOWS = (2, 4, 8, 16)
POOL_BUF = max(POOL_WINDOWS) - 1
POOL_HALO = 16
SSM_GROUP = 16
SSM_STATE = 64
HEAD_DIM = 128
MOBA_BLOCK = 256
MOBA_TOPK = 3
PAGE_SIZE = 128
PAGES_PER_BLOCK = MOBA_BLOCK // PAGE_SIZE
N_MEM_HEADS = 4
MEM_HEAD_DIM = 128
N_BRANCH = 3
LN_EPS = 1e-5
NEG_INF = -1e30
LANE = 128
VMEM_LIMIT = 56 * 1024 * 1024


def _params(sem, vmem=VMEM_LIMIT):
    return pltpu.CompilerParams(dimension_semantics=sem, vmem_limit_bytes=vmem)


def _dot(a, b):
    return jnp.dot(a.astype(BF16), b.astype(BF16), preferred_element_type=F32)


def _dot_nt(a, b):
    return lax.dot_general(a.astype(BF16), b.astype(BF16), (((1,), (1,)), ((), ())),
                           preferred_element_type=F32)


def _split(a):
    hi = a.astype(BF16)
    lo = (a - hi.astype(F32)).astype(BF16)
    return hi, lo


def _dot3(a, b):
    ah, al = _split(a)
    bh, bl = _split(b)
    d = functools.partial(jnp.dot, preferred_element_type=F32)
    return d(ah, bh) + (d(ah, bl) + d(al, bh))


def _dot3_nt(a, b):
    ah, al = _split(a)
    bh, bl = _split(b)
    d = functools.partial(lax.dot_general, dimension_numbers=(((1,), (1,)), ((), ())),
                          preferred_element_type=F32)
    return d(ah, bh) + (d(ah, bl) + d(al, bh))


def _layer_norm(y, g, b):
    mu = jnp.mean(y, axis=-1, keepdims=True)
    d = y - mu
    var = jnp.mean(d * d, axis=-1, keepdims=True)
    return d * lax.rsqrt(var + LN_EPS) * g + b


def _sigmoid(x):
    return 1.0 / (1.0 + jnp.exp(-x))


def _ffn_ln_kernel(x_ref, wg_ref, wu_ref, wd_ref, g_ref, b_ref, o_ref, xb_ref, acc_ref, *, alpha):
    f = pl.program_id(1)

    @pl.when(f == 0)
    def _():
        xb_ref[...] = x_ref[...].astype(BF16)
        acc_ref[...] = jnp.zeros_like(acc_ref)

    xb = xb_ref[...]
    gate = jnp.dot(xb, wg_ref[...].astype(BF16), preferred_element_type=F32)
    up = jnp.dot(xb, wu_ref[...].astype(BF16), preferred_element_type=F32)
    hid = gate * _sigmoid(gate) * up
    acc_ref[...] += _dot(hid, wd_ref[...])

    @pl.when(f == pl.num_programs(1) - 1)
    def _():
        y = alpha * x_ref[...] + 0.5 * acc_ref[...]
        o_ref[...] = _layer_norm(y, g_ref[...], b_ref[...])


def _ffn_ln(x, wg, wu, wd, ln_g, ln_b, l, ln_idx, *, alpha, tm, tf):
    m, d = x.shape
    f = wg.shape[2]
    return pl.pallas_call(
        functools.partial(_ffn_ln_kernel, alpha=alpha),
        grid=(m // tm, f // tf),
        in_specs=[
            pl.BlockSpec((tm, d), lambda i, j: (i, 0)),
            pl.BlockSpec((None, d, tf), lambda i, j: (l, 0, j)),
            pl.BlockSpec((None, d, tf), lambda i, j: (l, 0, j)),
            pl.BlockSpec((None, tf, d), lambda i, j: (l, j, 0)),
            pl.BlockSpec((None, 1, d), lambda i, j: (ln_idx, 0, 0)),
            pl.BlockSpec((None, 1, d), lambda i, j: (ln_idx, 0, 0)),
        ],
        out_specs=pl.BlockSpec((tm, d), lambda i, j: (i, 0)),
        out_shape=jax.ShapeDtypeStruct((m, d), F32),
        scratch_shapes=[pltpu.VMEM((tm, d), BF16), pltpu.VMEM((tm, d), F32)],
        compiler_params=_params(("parallel", "arbitrary")),
        name="ffn_ln",
    )(x, wg, wu, wd, ln_g, ln_b)


def _matmul_kernel(x_ref, w_ref, o_ref, xb_ref):
    @pl.when(pl.program_id(1) == 0)
    def _():
        xb_ref[...] = x_ref[...].astype(BF16)

    o_ref[...] = jnp.dot(xb_ref[...], w_ref[...].astype(BF16), preferred_element_type=F32)


def _matmul(x, w, l, *, tm, tn):
    m, k = x.shape
    n = w.shape[2]
    return pl.pallas_call(
        _matmul_kernel,
        grid=(m // tm, n // tn),
        in_specs=[
            pl.BlockSpec((tm, k), lambda i, j: (i, 0)),
            pl.BlockSpec((None, k, tn), lambda i, j: (l, 0, j)),
        ],
        out_specs=pl.BlockSpec((tm, tn), lambda i, j: (i, j)),
        out_shape=jax.ShapeDtypeStruct((m, n), F32),
        scratch_shapes=[pltpu.VMEM((tm, k), BF16)],
        compiler_params=_params(("parallel", "arbitrary")),
        name="matmul",
    )(x, w)


def _pool_kernel(u_ref, buf_ref, w_ref, scale_ref, o_ref, xx_ref, *, tl, start):
    ti = pl.program_id(1)

    @pl.when(ti == 0)
    def _():
        xx_ref[0:POOL_HALO, :] = buf_ref[...]

    u = u_ref[...]
    xx_ref[POOL_HALO:POOL_HALO + tl, :] = u
    pos = start + ti * tl + lax.broadcasted_iota(jnp.int32, (tl, 1), 0)
    for gi, w in enumerate(POOL_WINDOWS):
        cols = slice(gi * LANE, (gi + 1) * LANE)
        win = xx_ref[POOL_HALO:POOL_HALO + tl, cols]
        for i in range(1, w):
            win = win + xx_ref[POOL_HALO - i:POOL_HALO - i + tl, cols]
        cnt = jnp.minimum(pos + 1, w).astype(F32)
        pooled = win / cnt - u[:, cols]
        o_ref[:, cols] = _dot3(pooled, w_ref[gi]) * scale_ref[:, cols]
    xx_ref[0:POOL_HALO, :] = xx_ref[tl:tl + POOL_HALO, :]


def _pool(z, buf16, pool_w, pool_scale, l, *, tl, start):
    b, t, _ = z.shape
    w = buf16.shape[2]
    assert w == len(POOL_WINDOWS) * LANE
    return pl.pallas_call(
        functools.partial(_pool_kernel, tl=tl, start=start),
        grid=(b, t // tl),
        in_specs=[
            pl.BlockSpec((None, tl, w), lambda bi, ti: (bi, ti, 0)),
            pl.BlockSpec((None, POOL_HALO, w), lambda bi, ti: (bi, 0, 0)),
            pl.BlockSpec((None, len(POOL_WINDOWS), LANE, LANE), lambda bi, ti: (l, 0, 0, 0)),
            pl.BlockSpec((None, 1, w), lambda bi, ti: (l, 0, 0)),
        ],
        out_specs=pl.BlockSpec((None, tl, w), lambda bi, ti: (bi, ti, 0)),
        out_shape=jax.ShapeDtypeStruct((b, t, w), F32),
        scratch_shapes=[pltpu.VMEM((POOL_HALO + tl, w), F32)],
        compiler_params=_params(("parallel", "arbitrary")),
        name="pool",
    )(z, buf16, pool_w, pool_scale)


def _s5_kernel(u_ref, s0r_ref, s0i_ref, lr_ref, li_ref, ldt_ref, br_ref, bi_ref, cr_ref, ci_ref,
               d_ref, wg_ref, bg_ref, y_ref, sr_ref, si_ref,
               pr_ref, pi_ref, qr_ref, qi_ref, apr_ref, api_ref, zr_ref, zi_ref, car_ref, cai_ref,
               *, tl, pad, n_chunk):
    ti = pl.program_id(1)
    n_state = lr_ref.shape[1]
    n_pass = int(math.log2(tl))
    uc = u_ref.shape[1] // n_chunk
    sc = n_state // n_chunk

    @pl.when(ti == 0)
    def _():
        dt = jnp.exp(ldt_ref[...])
        lr, li = lr_ref[...], li_ref[...]
        mag = jnp.exp(lr * dt)
        a_re, a_im = mag * jnp.cos(li * dt), mag * jnp.sin(li * dt)
        den = lr * lr + li * li
        zr_ref[...] = ((a_re - 1.0) * lr + a_im * li) / den
        zi_ref[...] = (a_im * lr - (a_re - 1.0) * li) / den
        pw_re, pw_im = a_re, a_im
        for j in range(n_pass):
            apr_ref[j:j + 1, :] = pw_re
            api_ref[j:j + 1, :] = pw_im
            pw_re, pw_im = pw_re * pw_re - pw_im * pw_im, 2.0 * (pw_re * pw_im)
        car_ref[...] = s0r_ref[...]
        cai_ref[...] = s0i_ref[...]
        zeros = jnp.zeros((pad, n_state), F32)
        pr_ref[0:pad, :] = zeros
        pi_ref[0:pad, :] = zeros
        qr_ref[0:pad, :] = zeros
        qi_ref[0:pad, :] = zeros

    u = u_ref[...]
    for c in range(n_chunk):
        uu = u[:, c * uc:(c + 1) * uc]
        ub_re = _dot3(uu, br_ref[c])
        ub_im = _dot3(uu, bi_ref[c])
        cols = slice(c * sc, (c + 1) * sc)
        z_re, z_im = zr_ref[:, cols], zi_ref[:, cols]
        pr_ref[pad:pad + tl, cols] = z_re * ub_re - z_im * ub_im
        pi_ref[pad:pad + tl, cols] = z_re * ub_im + z_im * ub_re
    a_re, a_im = apr_ref[0:1, :], api_ref[0:1, :]
    c_re, c_im = car_ref[...], cai_ref[...]
    pr_ref[pad:pad + 1, :] += a_re * c_re - a_im * c_im
    pi_ref[pad:pad + 1, :] += a_re * c_im + a_im * c_re

    bufs = ((pr_ref, pi_ref), (qr_ref, qi_ref))
    for j in range(n_pass):
        k = 1 << j
        (s_re, s_im), (d_re, d_im) = bufs[j % 2], bufs[(j + 1) % 2]

        def col_body(cb, carry, k=k, j=j, s_re=s_re, s_im=s_im, d_re=d_re, d_im=d_im):
            cols = pl.ds(pl.multiple_of(cb * LANE, LANE), LANE)
            ar, ai = apr_ref[j:j + 1, cols], api_ref[j:j + 1, cols]
            cur_re, cur_im = s_re[pad:pad + tl, cols], s_im[pad:pad + tl, cols]
            sh_re, sh_im = s_re[pad - k:pad - k + tl, cols], s_im[pad - k:pad - k + tl, cols]
            d_re[pad:pad + tl, cols] = cur_re + (ar * sh_re - ai * sh_im)
            d_im[pad:pad + tl, cols] = cur_im + (ar * sh_im + ai * sh_re)
            return carry

        lax.fori_loop(0, n_state // LANE, col_body, 0)
    f_re, f_im = bufs[n_pass % 2]

    car_ref[...] = f_re[pad + tl - 1:pad + tl, :]
    cai_ref[...] = f_im[pad + tl - 1:pad + tl, :]
    sr_ref[...] = car_ref[...]
    si_ref[...] = cai_ref[...]

    ys = []
    for c in range(n_chunk):
        cols = slice(c * sc, (c + 1) * sc)
        ys.append(_dot3(f_re[pad:pad + tl, cols], cr_ref[c]) - _dot3(f_im[pad:pad + tl, cols], ci_ref[c]))
    y = jnp.concatenate(ys, axis=1) + d_ref[...] * u
    y = 0.5 * y * (1.0 + jnp.tanh(math.sqrt(2.0 / math.pi) * (y + 0.044715 * (y * y * y))))
    y_ref[...] = y * _sigmoid(_dot(y, wg_ref[...]) + bg_ref[...])


def _s5(z, col0, s0_re, s0_im, lam_re, lam_im, ldt_rep, b_re_c, b_im_c, c_re_c, c_im_c, ssm_d, glu_w, glu_b, l,
        *, tl):
    b, t, _ = z.shape
    n_state = s0_re.shape[2]
    n_chunk, uc, sc = b_re_c.shape[1:]
    u_w = n_chunk * uc
    assert col0 % u_w == 0 and n_chunk * sc == n_state
    pad = max(8, tl // 2)
    n_pass = int(math.log2(tl))
    assert 1 << n_pass == tl
    vec = lambda w: pl.BlockSpec((None, 1, w), lambda bi, ti: (l, 0, 0))
    state_spec = pl.BlockSpec((None, 1, n_state), lambda bi, ti: (bi, 0, 0))
    return pl.pallas_call(
        functools.partial(_s5_kernel, tl=tl, pad=pad, n_chunk=n_chunk),
        grid=(b, t // tl),
        in_specs=[
            pl.BlockSpec((None, tl, u_w), lambda bi, ti: (bi, ti, col0 // u_w)),
            state_spec, state_spec,
            vec(n_state), vec(n_state), vec(n_state),
            pl.BlockSpec((None, n_chunk, uc, sc), lambda bi, ti: (l, 0, 0, 0)),
            pl.BlockSpec((None, n_chunk, uc, sc), lambda bi, ti: (l, 0, 0, 0)),
            pl.BlockSpec((None, n_chunk, sc, uc), lambda bi, ti: (l, 0, 0, 0)),
            pl.BlockSpec((None, n_chunk, sc, uc), lambda bi, ti: (l, 0, 0, 0)),
            vec(u_w),
            pl.BlockSpec((None, u_w, u_w), lambda bi, ti: (l, 0, 0)),
            vec(u_w),
        ],
        out_specs=[
            pl.BlockSpec((None, tl, u_w), lambda bi, ti: (bi, ti, 0)),
            state_spec, state_spec,
        ],
        out_shape=[
            jax.ShapeDtypeStruct((b, t, u_w), F32),
            jax.ShapeDtypeStruct((b, 1, n_state), F32),
            jax.ShapeDtypeStruct((b, 1, n_state), F32),
        ],
        scratch_shapes=[pltpu.VMEM((pad + tl, n_state), F32)] * 4
        + [pltpu.VMEM((max(8, n_pass), n_state), F32)] * 2
        + [pltpu.VMEM((1, n_state), F32)] * 4,
        compiler_params=_params(("parallel", "arbitrary")),
        name="s5",
    )(z, s0_re, s0_im, lam_re, lam_im, ldt_rep, b_re_c, b_im_c, c_re_c, c_im_c, ssm_d, glu_w, glu_b)


def _block_diag_chunks(w, n_chunk):
    nl, g, r, c = w.shape
    gc = g // n_chunk
    w = w.reshape(nl, n_chunk, gc, r, c)
    eye = jnp.eye(gc, dtype=w.dtype)
    out = w[:, :, :, :, None, :] * eye[None, None, :, None, :, None]
    return out.reshape(nl, n_chunk, gc * r, gc * c)


def _top_blocks(gate, n_valid_col):
    lane = lax.broadcasted_iota(jnp.int32, gate.shape, 1)
    lane_f = lane.astype(F32)
    gate = jnp.where(lane < n_valid_col, gate, NEG_INF)
    sel = jnp.zeros(gate.shape, F32)
    for _ in range(MOBA_TOPK):
        m = jnp.max(gate, axis=1, keepdims=True)
        idx = jnp.min(jnp.where(gate == m, lane_f, float(gate.shape[1])), axis=1, keepdims=True)
        hit = lane_f == idx
        valid = jnp.broadcast_to(m, gate.shape) > 0.5 * NEG_INF
        sel = jnp.where(hit & valid, 1.0, sel)
        gate = jnp.where(hit, NEG_INF, gate)
    return sel


def _column(sel, j):
    lane = lax.broadcasted_iota(jnp.int32, sel.shape, 1)
    return jnp.sum(jnp.where(lane == j, sel, 0.0), axis=1, keepdims=True)


def _moba_prompt_kernel(slope_ref, q_ref, k_ref, v_ref, o_ref, kb_ref, vb_ref, km_ref, *, n_blk, scale):
    h = pl.program_id(1)
    n = pl.program_id(2)
    blk = MOBA_BLOCK

    @pl.when(n == 0)
    def _():
        kb_ref[...] = k_ref[...].astype(BF16)
        vb_ref[...] = v_ref[...].astype(BF16)
        km_ref[...] = jnp.zeros_like(km_ref)
        for j in range(n_blk):
            km_ref[j:j + 1, :] = jnp.mean(k_ref[j * blk:(j + 1) * blk, :], axis=0, keepdims=True)

    slope = slope_ref[h]
    q = q_ref[...]
    qb = q.astype(BF16)
    sel = _top_blocks(_dot3_nt(q, km_ref[...]), n)
    row = lax.broadcasted_iota(jnp.int32, (blk, blk), 0)
    col = lax.broadcasted_iota(jnp.int32, (blk, blk), 1)
    rel = (row - col).astype(F32)

    own0 = pl.multiple_of(n * blk, blk)
    s = _dot_nt(qb, kb_ref[pl.ds(own0, blk), :]) * scale
    s = jnp.where(row >= col, s - slope * rel, NEG_INF)
    m0 = jnp.max(s, axis=1, keepdims=True)
    p = jnp.exp(s - m0)
    l0 = jnp.sum(p, axis=1, keepdims=True)
    acc0 = _dot(p, vb_ref[pl.ds(own0, blk), :])

    def body(j, carry):
        m, lsum, acc = carry
        j0 = pl.multiple_of(j * blk, blk)
        chosen = jnp.broadcast_to(_column(sel, j), (blk, blk)) > 0.5
        s = _dot_nt(qb, kb_ref[pl.ds(j0, blk), :]) * scale
        dist = rel + ((n - j) * blk).astype(F32)
        s = jnp.where(chosen, s - slope * dist, NEG_INF)
        m_new = jnp.maximum(m, jnp.max(s, axis=1, keepdims=True))
        corr = jnp.exp(m - m_new)
        p = jnp.exp(s - m_new)
        lsum = corr * lsum + jnp.sum(p, axis=1, keepdims=True)
        acc = corr * acc + _dot(p, vb_ref[pl.ds(j0, blk), :])
        return m_new, lsum, acc

    _, lsum, acc = lax.fori_loop(0, n, body, (m0, l0, acc0))
    o_ref[...] = acc / lsum


def _moba_prompt(z, slopes, q_col, k_col, v_col, n_heads):
    b, t, _ = z.shape
    n_blk = t // MOBA_BLOCK
    assert n_blk * MOBA_BLOCK == t and n_blk <= LANE
    qc, kc, vc = q_col // HEAD_DIM, k_col // HEAD_DIM, v_col // HEAD_DIM
    return pl.pallas_call(
        functools.partial(_moba_prompt_kernel, n_blk=n_blk, scale=HEAD_DIM ** -0.5),
        grid_spec=pltpu.PrefetchScalarGridSpec(
            num_scalar_prefetch=1,
            grid=(b, n_heads, n_blk),
            in_specs=[
                pl.BlockSpec((None, MOBA_BLOCK, HEAD_DIM), lambda bi, h, n, s: (bi, n, qc + h)),
                pl.BlockSpec((None, t, HEAD_DIM), lambda bi, h, n, s: (bi, 0, kc + h)),
                pl.BlockSpec((None, t, HEAD_DIM), lambda bi, h, n, s: (bi, 0, vc + h)),
            ],
            out_specs=pl.BlockSpec((None, MOBA_BLOCK, HEAD_DIM), lambda bi, h, n, s: (bi, n, h)),
            scratch_shapes=[pltpu.VMEM((t, HEAD_DIM), BF16), pltpu.VMEM((t, HEAD_DIM), BF16),
                            pltpu.VMEM((LANE, HEAD_DIM), F32)],
        ),
        out_shape=jax.ShapeDtypeStruct((b, t, n_heads * HEAD_DIM), F32),
        compiler_params=_params(("parallel", "parallel", "arbitrary")),
        name="moba_prompt",
    )(slopes, z, z, z)


def _kmean_kernel(pt_ref, p0_ref, p1_ref, o_ref):
    j = pl.program_id(2)
    tot = jnp.sum(p0_ref[...], axis=0, keepdims=True) + jnp.sum(p1_ref[...], axis=0, keepdims=True)
    o_ref[pl.ds(j, 1), :] = tot * (1.0 / MOBA_BLOCK)


def _paged_block_means(cache, page_table):
    nl, _, page, w = cache.shape
    b, n_pages = page_table.shape
    assert page == PAGE_SIZE and PAGES_PER_BLOCK == 2
    n_blk = n_pages // PAGES_PER_BLOCK
    return pl.pallas_call(
        _kmean_kernel,
        grid_spec=pltpu.PrefetchScalarGridSpec(
            num_scalar_prefetch=1,
            grid=(nl, b, n_blk),
            in_specs=[
                pl.BlockSpec((None, None, page, w), lambda l, bi, j, pt: (l, pt[bi, 2 * j], 0, 0)),
                pl.BlockSpec((None, None, page, w), lambda l, bi, j, pt: (l, pt[bi, 2 * j + 1], 0, 0)),
            ],
            out_specs=pl.BlockSpec((None, None, n_blk, w), lambda l, bi, j, pt: (l, bi, 0, 0)),
        ),
        out_shape=jax.ShapeDtypeStruct((nl, b, n_blk, w), F32),
        compiler_params=_params(("parallel", "parallel", "arbitrary")),
        name="paged_block_means",
    )(page_table, cache, cache)


def _gate_sample_kernel(q_ref, km_ref, sel_ref, *, n_heads, n_blk):
    q = q_ref[...]
    km = km_ref[...]
    for h in range(n_heads):
        cols = slice(h * HEAD_DIM, (h + 1) * HEAD_DIM)
        km_h = jnp.concatenate([km[:, cols], jnp.zeros((LANE - n_blk, HEAD_DIM), F32)], axis=0)
        sel_ref[h] = _top_blocks(_dot3_nt(q[:, cols], km_h), n_blk)


def _gate_sample(z, q_col, kmeans, l, n_heads):
    b, t, _ = z.shape
    n_blk, w = kmeans.shape[2:]
    assert n_blk <= LANE
    return pl.pallas_call(
        functools.partial(_gate_sample_kernel, n_heads=n_heads, n_blk=n_blk),
        grid=(b,),
        in_specs=[
            pl.BlockSpec((None, t, w), lambda bi: (bi, 0, q_col // w)),
            pl.BlockSpec((None, None, n_blk, w), lambda bi: (l, bi, 0, 0)),
        ],
        out_specs=pl.BlockSpec((None, n_heads, t, LANE), lambda bi: (bi, 0, 0, 0)),
        out_shape=jax.ShapeDtypeStruct((b, n_heads, t, LANE), F32),
        compiler_params=_params(("parallel",)),
        name="gate_sample",
    )(z, kmeans)


def _moba_sample_kernel(pt_ref, blk_ref, cnt_ref, slope_ref, q_ref, kn_ref, vn_ref, sel_ref,
                        k0_ref, k1_ref, v0_ref, v1_ref, o_ref, m_ref, l_ref, acc_ref,
                        *, past_len, scale):
    bi, h, s = pl.program_id(0), pl.program_id(1), pl.program_id(2)
    t = q_ref.shape[0]
    slope = slope_ref[h]
    qb = q_ref[...].astype(BF16)

    row = lax.broadcasted_iota(jnp.int32, (t, PAGE_SIZE), 0)
    col = lax.broadcasted_iota(jnp.int32, (t, PAGE_SIZE), 1)

    @pl.when(s == 0)
    def _():
        fill = jnp.zeros((PAGE_SIZE - t, HEAD_DIM), F32)
        sc = _dot_nt(qb, jnp.concatenate([kn_ref[...], fill], axis=0)) * scale
        sc = jnp.where(row >= col, sc - slope * (row - col).astype(F32), NEG_INF)
        m0 = jnp.max(sc, axis=1, keepdims=True)
        p = jnp.exp(sc - m0)
        m_ref[...] = m0
        l_ref[...] = jnp.sum(p, axis=1, keepdims=True)
        acc_ref[...] = _dot(p, jnp.concatenate([vn_ref[...], fill], axis=0))

    @pl.when(s < cnt_ref[bi, h])
    def _():
        j = blk_ref[bi, h, s]
        chosen = jnp.broadcast_to(_column(sel_ref[...], j), (t, PAGE_SIZE)) > 0.5
        m, lsum, acc = m_ref[...], l_ref[...], acc_ref[...]
        for pg, (k_ref, v_ref) in enumerate(((k0_ref, v0_ref), (k1_ref, v1_ref))):
            dist = (past_len + row - col - (j * MOBA_BLOCK + pg * PAGE_SIZE)).astype(F32)
            sc = _dot_nt(qb, k_ref[...]) * scale
            sc = jnp.where(chosen, sc - slope * dist, NEG_INF)
            m_new = jnp.maximum(m, jnp.max(sc, axis=1, keepdims=True))
            corr = jnp.exp(m - m_new)
            p = jnp.exp(sc - m_new)
            lsum = corr * lsum + jnp.sum(p, axis=1, keepdims=True)
            acc = corr * acc + _dot(p, v_ref[...])
            m = m_new
        m_ref[...], l_ref[...], acc_ref[...] = m, lsum, acc

    @pl.when(s == pl.num_programs(2) - 1)
    def _():
        o_ref[...] = acc_ref[...] / l_ref[...]


def _moba_sample(z, sel, blk_list, blk_cnt, slopes, cache_k, cache_v, page_table, l, q_col, k_col, v_col,
                 n_heads, past_len):
    b, t, _ = z.shape
    n_slot = blk_list.shape[2]
    n_blk = sel.shape[3]
    qc, kc, vc = q_col // HEAD_DIM, k_col // HEAD_DIM, v_col // HEAD_DIM

    def page_spec(pg):
        return pl.BlockSpec(
            (None, None, PAGE_SIZE, HEAD_DIM),
            lambda bi, h, s, pt, bl, cn, sl: (l, pt[bi, 2 * bl[bi, h, s] + pg], 0, h))

    head_spec = lambda c: pl.BlockSpec((None, t, HEAD_DIM), lambda bi, h, s, pt, bl, cn, sl: (bi, 0, c + h))
    return pl.pallas_call(
        functools.partial(_moba_sample_kernel, past_len=past_len, scale=HEAD_DIM ** -0.5),
        grid_spec=pltpu.PrefetchScalarGridSpec(
            num_scalar_prefetch=4,
            grid=(b, n_heads, n_slot),
            in_specs=[
                head_spec(qc), head_spec(kc), head_spec(vc),
                pl.BlockSpec((None, None, t, n_blk), lambda bi, h, s, pt, bl, cn, sl: (bi, h, 0, 0)),
                page_spec(0), page_spec(1), page_spec(0), page_spec(1),
            ],
            out_specs=pl.BlockSpec((None, t, HEAD_DIM), lambda bi, h, s, pt, bl, cn, sl: (bi, 0, h)),
            scratch_shapes=[pltpu.VMEM((t, 1), F32), pltpu.VMEM((t, 1), F32), pltpu.VMEM((t, HEAD_DIM), F32)],
        ),
        out_shape=jax.ShapeDtypeStruct((b, t, n_heads * HEAD_DIM), F32),
        compiler_params=_params(("parallel", "parallel", "arbitrary")),
        name="moba_sample",
    )(page_table, blk_list, blk_cnt, slopes, z, z, z, sel, cache_k, cache_k, cache_v, cache_v)


def _selected_block_lists(sel, n_slot):
    anysel = jnp.max(sel, axis=2) > 0.5
    n_blk = anysel.shape[-1]
    cnt = jnp.sum(anysel, axis=-1).astype(jnp.int32)
    order = jnp.argsort(jnp.where(anysel, 0, 1) * n_blk + jnp.arange(n_blk), axis=-1)[..., :n_slot]
    slot = jnp.minimum(jnp.arange(n_slot), jnp.maximum(cnt[..., None] - 1, 0))
    return jnp.take_along_axis(order, slot, axis=-1).astype(jnp.int32), cnt


def _merge_ln_kernel(a_ref, s_ref, c_ref, g0_ref, g1_ref, g2_ref, bg0_ref, bg1_ref, bg2_ref,
                     wp_ref, ws_ref, wa_ref, wo_ref, h_ref, g_ref, b_ref, o_ref, acc_ref, *, alpha):
    j = pl.program_id(1)

    @pl.when(j == 0)
    def _():
        acc_ref[...] = jnp.zeros_like(acc_ref)

    merged = (_sigmoid(g0_ref[...] + bg0_ref[...]) * _dot(a_ref[...], wp_ref[...])
              + _sigmoid(g1_ref[...] + bg1_ref[...]) * _dot(s_ref[...], ws_ref[...])
              + _sigmoid(g2_ref[...] + bg2_ref[...]) * _dot(c_ref[...], wa_ref[...]))
    acc_ref[...] += _dot(merged, wo_ref[...])

    @pl.when(j == pl.num_programs(1) - 1)
    def _():
        o_ref[...] = _layer_norm(alpha * h_ref[...] + acc_ref[...], g_ref[...], b_ref[...])


def _merge_ln(a_out, s_out, c_out, z, gate_col, b_gate, wp, ws, wa, wo, h, ln_g, ln_b, l, ln_idx,
              *, alpha, tm, tn):
    m, d = h.shape
    gc = gate_col // tn
    nj = d // tn
    row = lambda w: pl.BlockSpec((tm, w), lambda i, j: (i, 0))
    gate = lambda g: pl.BlockSpec((tm, tn), lambda i, j: (i, gc + g * nj + j))
    bias = lambda g: pl.BlockSpec((None, 1, tn), lambda i, j: (l, 0, g * nj + j))
    wcol = lambda k: pl.BlockSpec((None, k, tn), lambda i, j: (l, 0, j))
    return pl.pallas_call(
        functools.partial(_merge_ln_kernel, alpha=alpha),
        grid=(m // tm, nj),
        in_specs=[
            row(a_out.shape[1]), row(s_out.shape[1]), row(c_out.shape[1]),
            gate(0), gate(1), gate(2), bias(0), bias(1), bias(2),
            wcol(a_out.shape[1]), wcol(s_out.shape[1]), wcol(c_out.shape[1]),
            pl.BlockSpec((None, tn, d), lambda i, j: (l, j, 0)),
            pl.BlockSpec((tm, d), lambda i, j: (i, 0)),
            pl.BlockSpec((None, 1, d), lambda i, j: (ln_idx, 0, 0)),
            pl.BlockSpec((None, 1, d), lambda i, j: (ln_idx, 0, 0)),
        ],
        out_specs=pl.BlockSpec((tm, d), lambda i, j: (i, 0)),
        out_shape=jax.ShapeDtypeStruct((m, d), F32),
        scratch_shapes=[pltpu.VMEM((tm, d), F32)],
        compiler_params=_params(("parallel", "arbitrary")),
        name="merge_ln",
    )(a_out, s_out, c_out, z, z, z, b_gate, b_gate, b_gate, wp, ws, wa, wo, h, ln_g, ln_b)


def _memattn_ln_kernel(h_ref, wq_ref, mk_ref, mv_ref, wo_ref, g_ref, b_ref, o_ref, *, alpha, scale):
    hx = h_ref[...]
    q = _dot(hx, wq_ref[...])
    mk, mv = mk_ref[...], mv_ref[...]
    outs = []
    for hd in range(N_MEM_HEADS):
        cols = slice(hd * MEM_HEAD_DIM, (hd + 1) * MEM_HEAD_DIM)
        s = _dot_nt(q[:, cols], mk[:, cols]) * scale
        s = s - jnp.max(s, axis=1, keepdims=True)
        p = jnp.exp(s)
        p = p / jnp.sum(p, axis=1, keepdims=True)
        outs.append(_dot(p, mv[:, cols]))
    o = jnp.concatenate(outs, axis=1)
    o_ref[...] = _layer_norm(alpha * hx + _dot(o, wo_ref[...]), g_ref[...], b_ref[...])


def _memattn_ln(h, mem_k, mem_v, mem_l, wq, wo, ln_g, ln_b, l, ln_idx, *, alpha, tm):
    b, t, d = h.shape
    mlen, w = mem_k.shape[2:]
    mem_spec = pl.BlockSpec((None, None, mlen, w), lambda bi, ti: (mem_l, bi, 0, 0))
    return pl.pallas_call(
        functools.partial(_memattn_ln_kernel, alpha=alpha, scale=MEM_HEAD_DIM ** -0.5),
        grid=(b, t // tm),
        in_specs=[
            pl.BlockSpec((None, tm, d), lambda bi, ti: (bi, ti, 0)),
            pl.BlockSpec((None, d, w), lambda bi, ti: (l, 0, 0)),
            mem_spec, mem_spec,
            pl.BlockSpec((None, w, d), lambda bi, ti: (l, 0, 0)),
            pl.BlockSpec((None, 1, d), lambda bi, ti: (ln_idx, 0, 0)),
            pl.BlockSpec((None, 1, d), lambda bi, ti: (ln_idx, 0, 0)),
        ],
        out_specs=pl.BlockSpec((None, tm, d), lambda bi, ti: (bi, ti, 0)),
        out_shape=jax.ShapeDtypeStruct((b, t, d), F32),
        compiler_params=_params(("parallel", "parallel")),
        name="memattn_ln",
    )(h, wq, mem_k, mem_v, wo, ln_g, ln_b)


def kernel(x_prompt, x_sample, cache_k, cache_v, cache_mem_k, cache_mem_v, state_pool, state_ssm_re, state_ssm_im, page_table, mem_prompt, ln_g, ln_b, ffn1_gate, ffn1_up, ffn1_down, w_in, b_gate, pool_w, pool_scale, ssm_lam_re, ssm_lam_im, ssm_log_dt, ssm_b_re, ssm_b_im, ssm_c_re, ssm_c_im, ssm_d, ssm_glu_w, ssm_glu_b, w_proj_pool, w_proj_ssm, w_proj_attn, w_out, mem_wq, mem_wk, mem_wv, mem_wo, ffn2_gate, ffn2_up, ffn2_down):
    depth = ln_g.shape[0]
    bp, tp, d = x_prompt.shape
    bs, ts, _ = x_sample.shape
    pool_width = state_pool.shape[3]
    n_groups, n_state = state_ssm_re.shape[2:]
    ssm_width = n_groups * SSM_GROUP
    n_heads = cache_k.shape[3]
    attn_width = n_heads * HEAD_DIM
    n_pages = page_table.shape[1]
    past_len = n_pages * PAGE_SIZE
    n_past_blk = n_pages // PAGES_PER_BLOCK
    mem_len = mem_prompt.shape[1]
    mem_width = N_MEM_HEADS * MEM_HEAD_DIM
    alpha = (2 * depth) ** 0.25
    ssm_col = pool_width
    q_col = ssm_col + ssm_width
    k_col = q_col + attn_width
    v_col = k_col + attn_width
    gate_col = v_col + attn_width

    ln_g2 = ln_g.reshape(depth * 4, 1, d)
    ln_b2 = ln_b.reshape(depth * 4, 1, d)
    b_gate2 = b_gate.reshape(depth, 1, N_BRANCH * d)
    pool_scale2 = pool_scale.reshape(depth, 1, pool_width)
    n_chunk = 4
    lam_re2 = ssm_lam_re.reshape(depth, 1, n_groups * n_state)
    lam_im2 = ssm_lam_im.reshape(depth, 1, n_groups * n_state)
    ldt_rep = jnp.repeat(ssm_log_dt, n_state, axis=1).reshape(depth, 1, n_groups * n_state)
    b_re_c = _block_diag_chunks(jnp.swapaxes(ssm_b_re, 2, 3), n_chunk)
    b_im_c = _block_diag_chunks(jnp.swapaxes(ssm_b_im, 2, 3), n_chunk)
    c_re_c = _block_diag_chunks(jnp.swapaxes(ssm_c_re, 2, 3), n_chunk)
    c_im_c = _block_diag_chunks(jnp.swapaxes(ssm_c_im, 2, 3), n_chunk)
    ssm_d2 = ssm_d.reshape(depth, 1, ssm_width)
    glu_b2 = ssm_glu_b.reshape(depth, 1, ssm_width)
    slopes = jnp.exp2(-8.0 * jnp.arange(1, n_heads + 1, dtype=F32) / n_heads)
    cache_k4 = cache_k.reshape(cache_k.shape[0], cache_k.shape[1], PAGE_SIZE, attn_width)
    cache_v4 = cache_v.reshape(cache_v.shape[0], cache_v.shape[1], PAGE_SIZE, attn_width)
    cmem_k = cache_mem_k.reshape(depth, bs, mem_len, mem_width)
    cmem_v = cache_mem_v.reshape(depth, bs, mem_len, mem_width)
    mem_rows = mem_prompt.reshape(bp * mem_len, d)
    zero_pool = jnp.zeros((bp, POOL_HALO, pool_width), F32)
    zero_state = jnp.zeros((bp, 1, n_groups * n_state), F32)
    state_pool16 = jnp.pad(state_pool, ((0, 0), (0, 0), (POOL_HALO - POOL_BUF, 0), (0, 0)))
    s_re_in = state_ssm_re.reshape(depth, bs, 1, n_groups * n_state)
    s_im_in = state_ssm_im.reshape(depth, bs, 1, n_groups * n_state)

    kmeans = _paged_block_means(cache_k4, page_table)
    n_slot = min(n_past_blk, MOBA_TOPK * ts)

    hp = x_prompt.reshape(bp * tp, d)
    hs = x_sample.reshape(bs * ts, d)
    mp, ms = bp * tp, bs * ts
    outs = {k: [] for k in ("kp", "vp", "poolp", "rep", "imp", "mkp", "mvp", "ks", "vs", "pools", "res", "ims")}

    def mix_common(h, z3, a_out, s_out, c_out, l, tm):
        m = h.shape[0]
        return _merge_ln(a_out.reshape(m, -1), s_out.reshape(m, -1), c_out.reshape(m, -1),
                         z3.reshape(m, -1), gate_col, b_gate2, w_proj_pool, w_proj_ssm, w_proj_attn, w_out,
                         h, ln_g2, ln_b2, l, 4 * l + 1, alpha=alpha, tm=tm, tn=512)

    for l in range(depth):
        mk_p = _matmul(mem_rows, mem_wk, l, tm=bp * mem_len, tn=mem_width)
        mv_p = _matmul(mem_rows, mem_wv, l, tm=bp * mem_len, tn=mem_width)
        hp = _ffn_ln(hp, ffn1_gate, ffn1_up, ffn1_down, ln_g2, ln_b2, l, 4 * l, alpha=alpha, tm=512, tf=512)
        zp = _matmul(hp, w_in, l, tm=1024, tn=512).reshape(bp, tp, -1)
        a_out = _pool(zp, zero_pool, pool_w, pool_scale2, l, tl=256, start=0)
        s_out, re_p, im_p = _s5(zp, ssm_col, zero_state, zero_state, lam_re2, lam_im2, ldt_rep,
                                b_re_c, b_im_c, c_re_c, c_im_c, ssm_d2, ssm_glu_w, glu_b2, l, tl=256)
        c_out = _moba_prompt(zp, slopes, q_col, k_col, v_col, n_heads)
        hp = mix_common(hp, zp, a_out, s_out, c_out, l, 512)
        hp = _memattn_ln(hp.reshape(bp, tp, d), mk_p.reshape(1, bp, mem_len, mem_width),
                         mv_p.reshape(1, bp, mem_len, mem_width), 0, mem_wq, mem_wo, ln_g2, ln_b2, l, 4 * l + 2,
                         alpha=alpha, tm=512).reshape(mp, d)
        hp = _ffn_ln(hp, ffn2_gate, ffn2_up, ffn2_down, ln_g2, ln_b2, l, 4 * l + 3, alpha=alpha, tm=512, tf=512)
        outs["kp"].append(zp[:, :, k_col:v_col].reshape(bp, tp, n_heads, HEAD_DIM))
        outs["vp"].append(zp[:, :, v_col:gate_col].reshape(bp, tp, n_heads, HEAD_DIM))
        outs["poolp"].append(zp[:, tp - POOL_BUF:, :pool_width])
        outs["rep"].append(re_p.reshape(bp, n_groups, n_state))
        outs["imp"].append(im_p.reshape(bp, n_groups, n_state))
        outs["mkp"].append(mk_p.reshape(bp, mem_len, N_MEM_HEADS, MEM_HEAD_DIM))
        outs["mvp"].append(mv_p.reshape(bp, mem_len, N_MEM_HEADS, MEM_HEAD_DIM))

        hs = _ffn_ln(hs, ffn1_gate, ffn1_up, ffn1_down, ln_g2, ln_b2, l, 4 * l, alpha=alpha, tm=ms, tf=512)
        zs = _matmul(hs, w_in, l, tm=ms, tn=512).reshape(bs, ts, -1)
        a_out = _pool(zs, state_pool16[l], pool_w, pool_scale2, l, tl=ts, start=past_len)
        s_out, re_s, im_s = _s5(zs, ssm_col, s_re_in[l], s_im_in[l], lam_re2, lam_im2, ldt_rep,
                                b_re_c, b_im_c, c_re_c, c_im_c, ssm_d2, ssm_glu_w, glu_b2, l, tl=ts)
        sel = _gate_sample(zs, q_col, kmeans, l, n_heads)
        blk_list, blk_cnt = _selected_block_lists(sel[..., :n_past_blk], n_slot)
        c_out = _moba_sample(zs, sel, blk_list, blk_cnt, slopes, cache_k4, cache_v4, page_table, l,
                             q_col, k_col, v_col, n_heads, past_len)
        hs = mix_common(hs, zs, a_out, s_out, c_out, l, ms)
        hs = _memattn_ln(hs.reshape(bs, ts, d), cmem_k, cmem_v, l, mem_wq, mem_wo, ln_g2, ln_b2, l, 4 * l + 2,
                         alpha=alpha, tm=ts).reshape(ms, d)
        hs = _ffn_ln(hs, ffn2_gate, ffn2_up, ffn2_down, ln_g2, ln_b2, l, 4 * l + 3, alpha=alpha, tm=ms, tf=512)
        outs["ks"].append(zs[:, :, k_col:v_col].reshape(bs, ts, n_heads, HEAD_DIM))
        outs["vs"].append(zs[:, :, v_col:gate_col].reshape(bs, ts, n_heads, HEAD_DIM))
        outs["pools"].append(jnp.concatenate([state_pool[l], zs[:, :, :pool_width]], axis=1)[:, -POOL_BUF:])
        outs["res"].append(re_s.reshape(bs, n_groups, n_state))
        outs["ims"].append(im_s.reshape(bs, n_groups, n_state))

    st = lambda k: jnp.stack(outs[k])
    return (hp.reshape(bp, tp, d), hs.reshape(bs, ts, d),
            st("kp"), st("vp"), st("poolp"), st("rep"), st("imp"), st("mkp"), st("mvp"),
            st("ks"), st("vs"), st("pools"), st("res"), st("ims"))
```

```python
import functools
import math

import jax
import jax.numpy as jnp
from jax import lax
from jax.experimental import pallas as pl
from jax.experimental.pallas import tpu as pltpu

F32 = jnp.float32
BF16 = jnp.bfloat16

POOL_WINDOWS = (2, 4, 8, 16)
POOL_BUF = max(POOL_WINDOWS) - 1
POOL_HALO = 16
SSM_GROUP = 16
SSM_STATE = 64
HEAD_DIM = 128
MOBA_BLOCK = 256
MOBA_TOPK = 3
PAGE_SIZE = 128
PAGES_PER_BLOCK = MOBA_BLOCK // PAGE_SIZE
N_MEM_HEADS = 4
MEM_HEAD_DIM = 128
N_BRANCH = 3
LN_EPS = 1e-5
NEG_INF = -1e30
LANE = 128
VMEM_LIMIT = 56 * 1024 * 1024


def _params(sem, vmem=VMEM_LIMIT):
    return pltpu.CompilerParams(dimension_semantics=sem, vmem_limit_bytes=vmem)


def _dot(a, b):
    return jnp.dot(a.astype(BF16), b.astype(BF16), preferred_element_type=F32)


def _dot_nt(a, b):
    return lax.dot_general(a.astype(BF16), b.astype(BF16), (((1,), (1,)), ((), ())),
                           preferred_element_type=F32)


def _split(a):
    hi = a.astype(BF16)
    lo = (a - hi.astype(F32)).astype(BF16)
    return hi, lo


def _dot3(a, b):
    ah, al = _split(a)
    bh, bl = _split(b)
    d = functools.partial(jnp.dot, preferred_element_type=F32)
    return d(ah, bh) + (d(ah, bl) + d(al, bh))


def _dot3_nt(a, b):
    ah, al = _split(a)
    bh, bl = _split(b)
    d = functools.partial(lax.dot_general, dimension_numbers=(((1,), (1,)), ((), ())),
                          preferred_element_type=F32)
    return d(ah, bh) + (d(ah, bl) + d(al, bh))


def _layer_norm(y, g, b):
    mu = jnp.mean(y, axis=-1, keepdims=True)
    d = y - mu
    var = jnp.mean(d * d, axis=-1, keepdims=True)
    return d * lax.rsqrt(var + LN_EPS) * g + b


def _sigmoid(x):
    return 1.0 / (1.0 + jnp.exp(-x))


def _ffn_ln_kernel(x_ref, wg_ref, wu_ref, wd_ref, g_ref, b_ref, o_ref, xb_ref, *, alpha):
    f = pl.program_id(1)

    @pl.when(f == 0)
    def _():
        xb_ref[...] = x_ref[...].astype(BF16)
        o_ref[...] = jnp.zeros_like(o_ref)

    xb = xb_ref[...]
    gate = jnp.dot(xb, wg_ref[...].astype(BF16), preferred_element_type=F32)
    up = jnp.dot(xb, wu_ref[...].astype(BF16), preferred_element_type=F32)
    hid = gate * _sigmoid(gate) * up
    o_ref[...] += _dot(hid, wd_ref[...])

    @pl.when(f == pl.num_programs(1) - 1)
    def _():
        y = alpha * x_ref[...] + 0.5 * o_ref[...]
        o_ref[...] = _layer_norm(y, g_ref[...], b_ref[...])


def _ffn_ln(x, wg, wu, wd, ln_g, ln_b, l, ln_idx, *, alpha, tm, tf):
    m, d = x.shape
    f = wg.shape[2]
    return pl.pallas_call(
        functools.partial(_ffn_ln_kernel, alpha=alpha),
        grid=(m // tm, f // tf),
        in_specs=[
            pl.BlockSpec((tm, d), lambda i, j: (i, 0), pipeline_mode=pl.Buffered(1)),
            pl.BlockSpec((None, d, tf), lambda i, j: (l, 0, j)),
            pl.BlockSpec((None, d, tf), lambda i, j: (l, 0, j)),
            pl.BlockSpec((None, tf, d), lambda i, j: (l, j, 0)),
            pl.BlockSpec((None, 1, d), lambda i, j: (ln_idx, 0, 0)),
            pl.BlockSpec((None, 1, d), lambda i, j: (ln_idx, 0, 0)),
        ],
        out_specs=pl.BlockSpec((tm, d), lambda i, j: (i, 0), pipeline_mode=pl.Buffered(1)),
        out_shape=jax.ShapeDtypeStruct((m, d), F32),
        scratch_shapes=[pltpu.VMEM((tm, d), BF16)],
        compiler_params=_params(("parallel", "arbitrary")),
        name="ffn_ln",
    )(x, wg, wu, wd, ln_g, ln_b)


def _matmul_kernel(x_ref, w_ref, o_ref, xb_ref):
    @pl.when(pl.program_id(1) == 0)
    def _():
        xb_ref[...] = x_ref[...].astype(BF16)

    o_ref[...] = jnp.dot(xb_ref[...], w_ref[...].astype(BF16), preferred_element_type=F32)


def _matmul(x, w, l, *, tm, tn):
    m, k = x.shape
    n = w.shape[2]
    return pl.pallas_call(
        _matmul_kernel,
        grid=(m // tm, n // tn),
        in_specs=[
            pl.BlockSpec((tm, k), lambda i, j: (i, 0), pipeline_mode=pl.Buffered(1)),
            pl.BlockSpec((None, k, tn), lambda i, j: (l, 0, j)),
        ],
        out_specs=pl.BlockSpec((tm, tn), lambda i, j: (i, j)),
        out_shape=jax.ShapeDtypeStruct((m, n), F32),
        scratch_shapes=[pltpu.VMEM((tm, k), BF16)],
        compiler_params=_params(("parallel", "arbitrary")),
        name="matmul",
    )(x, w)


def _pool_kernel(u_ref, buf_ref, w_ref, scale_ref, o_ref, xx_ref, *, tl, start):
    ti = pl.program_id(1)

    @pl.when(ti == 0)
    def _():
        xx_ref[0:POOL_HALO, :] = buf_ref[...]

    u = u_ref[...]
    xx_ref[POOL_HALO:POOL_HALO + tl, :] = u
    pos = start + ti * tl + lax.broadcasted_iota(jnp.int32, (tl, 1), 0)
    for gi, w in enumerate(POOL_WINDOWS):
        cols = slice(gi * LANE, (gi + 1) * LANE)
        win = xx_ref[POOL_HALO:POOL_HALO + tl, cols]
        for i in range(1, w):
            win = win + xx_ref[POOL_HALO - i:POOL_HALO - i + tl, cols]
        cnt = jnp.minimum(pos + 1, w).astype(F32)
        pooled = win / cnt - u[:, cols]
        o_ref[:, cols] = _dot3(pooled, w_ref[gi]) * scale_ref[:, cols]
    xx_ref[0:POOL_HALO, :] = xx_ref[tl:tl + POOL_HALO, :]


def _pool(z, buf16, pool_w, pool_scale, l, *, tl, start):
    b, t, _ = z.shape
    w = buf16.shape[2]
    assert w == len(POOL_WINDOWS) * LANE
    return pl.pallas_call(
        functools.partial(_pool_kernel, tl=tl, start=start),
        grid=(b, t // tl),
        in_specs=[
            pl.BlockSpec((None, tl, w), lambda bi, ti: (bi, ti, 0)),
            pl.BlockSpec((None, POOL_HALO, w), lambda bi, ti: (bi, 0, 0)),
            pl.BlockSpec((None, len(POOL_WINDOWS), LANE, LANE), lambda bi, ti: (l, 0, 0, 0)),
            pl.BlockSpec((None, 1, w), lambda bi, ti: (l, 0, 0)),
        ],
        out_specs=pl.BlockSpec((None, tl, w), lambda bi, ti: (bi, ti, 0)),
        out_shape=jax.ShapeDtypeStruct((b, t, w), F32),
        scratch_shapes=[pltpu.VMEM((POOL_HALO + tl, w), F32)],
        compiler_params=_params(("parallel", "arbitrary")),
        name="pool",
    )(z, buf16, pool_w, pool_scale)


def _s5_kernel(u_ref, s0r_ref, s0i_ref, lr_ref, li_ref, ldt_ref, br_ref, bi_ref, cr_ref, ci_ref,
               d_ref, wg_ref, bg_ref, y_ref, sr_ref, si_ref,
               pr_ref, pi_ref, qr_ref, qi_ref, apr_ref, api_ref, zr_ref, zi_ref, car_ref, cai_ref,
               *, tl, pad, n_chunk):
    ti = pl.program_id(1)
    n_state = lr_ref.shape[1]
    n_pass = int(math.log2(tl))
    uc = u_ref.shape[1] // n_chunk
    sc = n_state // n_chunk

    @pl.when(ti == 0)
    def _():
        dt = jnp.exp(ldt_ref[...])
        lr, li = lr_ref[...], li_ref[...]
        mag = jnp.exp(lr * dt)
        a_re, a_im = mag * jnp.cos(li * dt), mag * jnp.sin(li * dt)
        den = lr * lr + li * li
        zr_ref[...] = ((a_re - 1.0) * lr + a_im * li) / den
        zi_ref[...] = (a_im * lr - (a_re - 1.0) * li) / den
        pw_re, pw_im = a_re, a_im
        for j in range(n_pass):
            apr_ref[j:j + 1, :] = pw_re
            api_ref[j:j + 1, :] = pw_im
            pw_re, pw_im = pw_re * pw_re - pw_im * pw_im, 2.0 * (pw_re * pw_im)
        car_ref[...] = s0r_ref[...]
        cai_ref[...] = s0i_ref[...]
        zeros = jnp.zeros((pad, n_state), F32)
        pr_ref[0:pad, :] = zeros
        pi_ref[0:pad, :] = zeros
        qr_ref[0:pad, :] = zeros
        qi_ref[0:pad, :] = zeros

    u = u_ref[...]
    for c in range(n_chunk):
        uu = u[:, c * uc:(c + 1) * uc]
        ub_re = _dot3(uu, br_ref[c])
        ub_im = _dot3(uu, bi_ref[c])
        cols = slice(c * sc, (c + 1) * sc)
        z_re, z_im = zr_ref[:, cols], zi_ref[:, cols]
        pr_ref[pad:pad + tl, cols] = z_re * ub_re - z_im * ub_im
        pi_ref[pad:pad + tl, cols] = z_re * ub_im + z_im * ub_re
    a_re, a_im = apr_ref[0:1, :], api_ref[0:1, :]
    c_re, c_im = car_ref[...], cai_ref[...]
    pr_ref[pad:pad + 1, :] += a_re * c_re - a_im * c_im
    pi_ref[pad:pad + 1, :] += a_re * c_im + a_im * c_re

    bufs = ((pr_ref, pi_ref), (qr_ref, qi_ref))
    for j in range(n_pass):
        k = 1 << j
        (s_re, s_im), (d_re, d_im) = bufs[j % 2], bufs[(j + 1) % 2]

        def col_body(cb, carry, k=k, j=j, s_re=s_re, s_im=s_im, d_re=d_re, d_im=d_im):
            cols = pl.ds(pl.multiple_of(cb * LANE, LANE), LANE)
            ar, ai = apr_ref[j:j + 1, cols], api_ref[j:j + 1, cols]
            cur_re, cur_im = s_re[pad:pad + tl, cols], s_im[pad:pad + tl, cols]
            sh_re, sh_im = s_re[pad - k:pad - k + tl, cols], s_im[pad - k:pad - k + tl, cols]
            d_re[pad:pad + tl, cols] = cur_re + (ar * sh_re - ai * sh_im)
            d_im[pad:pad + tl, cols] = cur_im + (ar * sh_im + ai * sh_re)
            return carry

        lax.fori_loop(0, n_state // LANE, col_body, 0)
    f_re, f_im = bufs[n_pass % 2]

    car_ref[...] = f_re[pad + tl - 1:pad + tl, :]
    cai_ref[...] = f_im[pad + tl - 1:pad + tl, :]
    sr_ref[...] = car_ref[...]
    si_ref[...] = cai_ref[...]

    ys = []
    for c in range(n_chunk):
        cols = slice(c * sc, (c + 1) * sc)
        ys.append(_dot3(f_re[pad:pad + tl, cols], cr_ref[c]) - _dot3(f_im[pad:pad + tl, cols], ci_ref[c]))
    y = jnp.concatenate(ys, axis=1) + d_ref[...] * u
    y = 0.5 * y * (1.0 + jnp.tanh(math.sqrt(2.0 / math.pi) * (y + 0.044715 * (y * y * y))))
    y_ref[...] = y * _sigmoid(_dot(y, wg_ref[...]) + bg_ref[...])


def _s5(z, col0, s0_re, s0_im, lam_re, lam_im, ldt_rep, b_re_c, b_im_c, c_re_c, c_im_c, ssm_d, glu_w, glu_b, l,
        *, tl):
    b, t, _ = z.shape
    n_state = s0_re.shape[2]
    n_chunk, uc, sc = b_re_c.shape[1:]
    u_w = n_chunk * uc
    assert col0 % u_w == 0 and n_chunk * sc == n_state
    pad = max(8, tl // 2)
    n_pass = int(math.log2(tl))
    assert 1 << n_pass == tl
    vec = lambda w: pl.BlockSpec((None, 1, w), lambda bi, ti: (l, 0, 0))
    state_spec = pl.BlockSpec((None, 1, n_state), lambda bi, ti: (bi, 0, 0))
    return pl.pallas_call(
        functools.partial(_s5_kernel, tl=tl, pad=pad, n_chunk=n_chunk),
        grid=(b, t // tl),
        in_specs=[
            pl.BlockSpec((None, tl, u_w), lambda bi, ti: (bi, ti, col0 // u_w)),
            state_spec, state_spec,
            vec(n_state), vec(n_state), vec(n_state),
            pl.BlockSpec((None, n_chunk, uc, sc), lambda bi, ti: (l, 0, 0, 0)),
            pl.BlockSpec((None, n_chunk, uc, sc), lambda bi, ti: (l, 0, 0, 0)),
            pl.BlockSpec((None, n_chunk, sc, uc), lambda bi, ti: (l, 0, 0, 0)),
            pl.BlockSpec((None, n_chunk, sc, uc), lambda bi, ti: (l, 0, 0, 0)),
            vec(u_w),
            pl.BlockSpec((None, u_w, u_w), lambda bi, ti: (l, 0, 0)),
            vec(u_w),
        ],
        out_specs=[
            pl.BlockSpec((None, tl, u_w), lambda bi, ti: (bi, ti, 0)),
            state_spec, state_spec,
        ],
        out_shape=[
            jax.ShapeDtypeStruct((b, t, u_w), F32),
            jax.ShapeDtypeStruct((b, 1, n_state), F32),
            jax.ShapeDtypeStruct((b, 1, n_state), F32),
        ],
        scratch_shapes=[pltpu.VMEM((pad + tl, n_state), F32)] * 4
        + [pltpu.VMEM((max(8, n_pass), n_state), F32)] * 2
        + [pltpu.VMEM((1, n_state), F32)] * 4,
        compiler_params=_params(("parallel", "arbitrary")),
        name="s5",
    )(z, s0_re, s0_im, lam_re, lam_im, ldt_rep, b_re_c, b_im_c, c_re_c, c_im_c, ssm_d, glu_w, glu_b)


def _block_diag_chunks(w, n_chunk):
    nl, g, r, c = w.shape
    gc = g // n_chunk
    w = w.reshape(nl, n_chunk, gc, r, c)
    eye = jnp.eye(gc, dtype=w.dtype)
    out = w[:, :, :, :, None, :] * eye[None, None, :, None, :, None]
    return out.reshape(nl, n_chunk, gc * r, gc * c)


def _top_blocks(gate, n_valid_col):
    lane = lax.broadcasted_iota(jnp.int32, gate.shape, 1)
    lane_f = lane.astype(F32)
    gate = jnp.where(lane < n_valid_col, gate, NEG_INF)
    sel = jnp.zeros(gate.shape, F32)
    for _ in range(MOBA_TOPK):
        m = jnp.max(gate, axis=1, keepdims=True)
        idx = jnp.min(jnp.where(gate == m, lane_f, float(gate.shape[1])), axis=1, keepdims=True)
        hit = lane_f == idx
        valid = jnp.broadcast_to(m, gate.shape) > 0.5 * NEG_INF
        sel = jnp.where(hit & valid, 1.0, sel)
        gate = jnp.where(hit, NEG_INF, gate)
    return sel


def _pick_column(sel_b, j):
    k = lax.broadcasted_iota(jnp.int32, (LANE, LANE), 0)
    onehot = jnp.where(k == j, 1.0, 0.0).astype(BF16)
    return jnp.dot(sel_b, onehot, preferred_element_type=F32) > 0.5


def _lane_tile(x, k):
    return x if k == 1 else jnp.concatenate([x] * k, axis=1)


def _moba_prompt_kernel(slope_ref, q_ref, k_ref, v_ref, o_ref, kb_ref, vb_ref, km_ref, *, n_blk, hp, cb, scale):
    hg = pl.program_id(1)
    n = pl.program_id(2)
    blk = MOBA_BLOCK
    wc = cb * blk
    rows = hp * blk
    t = k_ref.shape[0]
    stack = lambda per_head: jnp.concatenate(per_head, axis=0)
    head_rows = lambda x, hh: x[hh * blk:(hh + 1) * blk]

    @pl.when(n == 0)
    def _():
        km_ref[...] = jnp.zeros_like(km_ref)
        for hh in range(hp):
            cols = slice(hh * HEAD_DIM, (hh + 1) * HEAD_DIM)
            kb_ref[hh] = k_ref[:, cols].astype(BF16)
            vb_ref[hh, :, 0:HEAD_DIM] = v_ref[:, cols].astype(BF16)
            vb_ref[hh, :, HEAD_DIM:] = jnp.ones((t, HEAD_DIM), BF16)
            for j in range(n_blk):
                km_ref[hh, j:j + 1, :] = jnp.mean(k_ref[j * blk:(j + 1) * blk, cols], axis=0, keepdims=True)

    row = lax.broadcasted_iota(jnp.int32, (blk, wc), 0)
    col = lax.broadcasted_iota(jnp.int32, (blk, wc), 1)
    rel = (row - col).astype(F32)
    slopes = [slope_ref[hg * hp + hh] for hh in range(hp)]
    base = stack([-s * rel for s in slopes])
    slope_rows = stack([jnp.zeros((blk, LANE), F32) + s for s in slopes])
    qs = [q_ref[:, hh * HEAD_DIM:(hh + 1) * HEAD_DIM] for hh in range(hp)]
    qbs = [(q * scale).astype(BF16) for q in qs]
    gate = stack([_dot3_nt(q, km_ref[hh]) for hh, q in enumerate(qs)])
    sel_b = _top_blocks(gate, n).astype(BF16)
    rowmax = lambda x: jnp.broadcast_to(jnp.max(x, axis=1, keepdims=True), (rows, LANE))

    own0 = pl.multiple_of(n * blk, blk)
    causal = stack([lax.broadcasted_iota(jnp.int32, (blk, blk), 0)
                    >= lax.broadcasted_iota(jnp.int32, (blk, blk), 1)] * hp)
    s = stack([_dot_nt(qb, kb_ref[hh, pl.ds(own0, blk), :]) for hh, qb in enumerate(qbs)]) + base[:, :blk]
    s = jnp.where(causal, s, NEG_INF)
    m0 = rowmax(s)
    p = jnp.exp(s - _lane_tile(m0, blk // LANE))
    acc0 = stack([_dot(head_rows(p, hh), vb_ref[hh, pl.ds(own0, blk), :]) for hh in range(hp)])

    def body(c, carry):
        m, acc = carry
        c0 = pl.multiple_of(c * wc, wc)
        shift = slope_rows * (n * blk - c * wc).astype(F32)
        s = stack([_dot_nt(qb, kb_ref[hh, pl.ds(c0, wc), :]) for hh, qb in enumerate(qbs)]) + base
        groups = [s[:, g * blk:(g + 1) * blk] for g in range(cb)]
        chosen = [_pick_column(sel_b, c * cb + g) for g in range(cb)]
        m_new = m
        for g in range(cb):
            m_new = jnp.maximum(m_new, jnp.where(chosen[g], rowmax(groups[g]) - shift, NEG_INF))
        p = jnp.concatenate(
            [jnp.exp(groups[g] - _lane_tile(jnp.where(chosen[g], m_new + shift, -NEG_INF), blk // LANE))
             for g in range(cb)], axis=1)
        pv = stack([_dot(head_rows(p, hh), vb_ref[hh, pl.ds(c0, wc), :]) for hh in range(hp)])
        return m_new, _lane_tile(jnp.exp(m - m_new), 2 * HEAD_DIM // LANE) * acc + pv

    _, acc = lax.fori_loop(0, (n + cb - 1) // cb, body, (m0, acc0))
    for hh in range(hp):
        a = head_rows(acc, hh)
        o_ref[:, hh * HEAD_DIM:(hh + 1) * HEAD_DIM] = a[:, :HEAD_DIM] / a[:, HEAD_DIM:]


def _moba_prompt(z, slopes, q_col, k_col, v_col, n_heads, *, hp, cb):
    b, t, _ = z.shape
    n_blk = t // MOBA_BLOCK
    w = hp * HEAD_DIM
    assert n_blk * MOBA_BLOCK == t and n_blk <= LANE and n_heads % hp == 0 and n_blk % cb == 0
    assert q_col % w == 0 and k_col % w == 0 and v_col % w == 0
    qc, kc, vc = q_col // w, k_col // w, v_col // w
    whole = lambda c: pl.BlockSpec((None, t, w), lambda bi, hg, n, s: (bi, 0, c + hg),
                                   pipeline_mode=pl.Buffered(1))
    return pl.pallas_call(
        functools.partial(_moba_prompt_kernel, n_blk=n_blk, hp=hp, cb=cb, scale=HEAD_DIM ** -0.5),
        grid_spec=pltpu.PrefetchScalarGridSpec(
            num_scalar_prefetch=1,
            grid=(b, n_heads // hp, n_blk),
            in_specs=[
                pl.BlockSpec((None, MOBA_BLOCK, w), lambda bi, hg, n, s: (bi, n, qc + hg)),
                whole(kc), whole(vc),
            ],
            out_specs=pl.BlockSpec((None, MOBA_BLOCK, w), lambda bi, hg, n, s: (bi, n, hg)),
            scratch_shapes=[pltpu.VMEM((hp, t, HEAD_DIM), BF16), pltpu.VMEM((hp, t, 2 * HEAD_DIM), BF16),
                            pltpu.VMEM((hp, LANE, HEAD_DIM), F32)],
        ),
        out_shape=jax.ShapeDtypeStruct((b, t, n_heads * HEAD_DIM), F32),
        compiler_params=_params(("parallel", "parallel", "arbitrary")),
        name="moba_prompt",
    )(slopes, z, z, z)


def _page_spec(n_heads, index_map):
    return pl.BlockSpec((None, None, PAGE_SIZE * n_heads, HEAD_DIM), index_map)


def _kmean_kernel(pt_ref, *refs, n_heads, bps):
    pages, o_ref = refs[:PAGES_PER_BLOCK * bps], refs[PAGES_PER_BLOCK * bps]
    jj = pl.program_id(2)
    for bi in range(bps):
        tot = jnp.zeros((n_heads, HEAD_DIM), F32)
        for pg in range(PAGES_PER_BLOCK):
            page = pages[PAGES_PER_BLOCK * bi + pg][...].reshape(PAGE_SIZE, n_heads, HEAD_DIM)
            tot = tot + jnp.sum(page, axis=0)
        row0 = pl.multiple_of((jj * bps + bi) * n_heads, n_heads)
        o_ref[pl.ds(row0, n_heads), :] = tot * (1.0 / MOBA_BLOCK)


def _paged_block_means(cache, page_table, n_heads, *, bps):
    nl = cache.shape[0]
    b, n_pages = page_table.shape
    n_blk = n_pages // PAGES_PER_BLOCK
    assert n_blk % bps == 0

    def spec(k):
        return _page_spec(n_heads, lambda l, bi, jj, pt: (l, pt[bi, PAGES_PER_BLOCK * bps * jj + k], 0, 0))

    return pl.pallas_call(
        functools.partial(_kmean_kernel, n_heads=n_heads, bps=bps),
        grid_spec=pltpu.PrefetchScalarGridSpec(
            num_scalar_prefetch=1,
            grid=(nl, b, n_blk // bps),
            in_specs=[spec(k) for k in range(PAGES_PER_BLOCK * bps)],
            out_specs=pl.BlockSpec((None, None, n_blk * n_heads, HEAD_DIM), lambda l, bi, jj, pt: (l, bi, 0, 0)),
        ),
        out_shape=jax.ShapeDtypeStruct((nl, b, n_blk * n_heads, HEAD_DIM), F32),
        compiler_params=_params(("parallel", "parallel", "arbitrary")),
        name="paged_block_means",
    )(page_table, *([cache] * (PAGES_PER_BLOCK * bps)))


def _head_major(page_ref, n_heads):
    return jnp.concatenate([page_ref[pl.ds(h, PAGE_SIZE, stride=n_heads), :] for h in range(n_heads)], axis=1)


def _moba_sample_kernel(pt_ref, slope_ref, q_ref, kn_ref, vn_ref, km_ref, *refs,
                        n_heads, n_blk, bps, past_len, scale):
    npg = PAGES_PER_BLOCK * bps
    k_refs, v_refs = refs[:npg], refs[npg:2 * npg]
    o_ref, qbd_ref, sel_ref, slp_ref, m_ref, l_ref, acc_ref = refs[2 * npg:]
    jj = pl.program_id(1)
    t, width = q_ref.shape
    rows = LANE
    rid = lax.broadcasted_iota(jnp.int32, (rows, 1), 0).astype(F32)
    hq = jnp.floor((rid + 0.5) * (1.0 / t))
    tq = rid - t * hq

    @pl.when(jj == 0)
    def _():
        q = q_ref[...]
        c = lax.broadcasted_iota(jnp.int32, (t, width), 1)
        per_head = [jnp.where((c >= h * HEAD_DIM) & (c < (h + 1) * HEAD_DIM), q, 0.0) for h in range(n_heads)]
        qbd = jnp.concatenate(per_head + [jnp.zeros((rows - n_heads * t, width), F32)], axis=0)
        qbd_ref[...] = (qbd * scale).astype(BF16)
        gates = []
        for h in range(n_heads):
            km_h = km_ref[pl.ds(h, n_blk, stride=n_heads), :]
            km_h = jnp.concatenate([km_h, jnp.zeros((LANE - n_blk, HEAD_DIM), F32)], axis=0)
            gates.append(_dot3_nt(q[:, h * HEAD_DIM:(h + 1) * HEAD_DIM], km_h))
        gate = jnp.concatenate(gates + [jnp.zeros((rows - n_heads * t, LANE), F32)], axis=0)
        sel_ref[...] = _top_blocks(gate, n_blk).astype(BF16)
        slp = jnp.zeros((rows, 1), F32)
        for h in range(n_heads):
            slp = jnp.where(hq == h, slope_ref[h], slp)
        slp_ref[...] = slp
        fill = jnp.zeros((LANE - t, width), F32)
        s = _dot_nt(qbd_ref[...], jnp.concatenate([kn_ref[...], fill], axis=0))
        cpos = lax.broadcasted_iota(jnp.int32, (rows, LANE), 1).astype(F32)
        s = jnp.where(cpos <= tq, s - slp * (tq - cpos), NEG_INF)
        m0 = jnp.max(s, axis=1, keepdims=True)
        p = jnp.exp(s - m0)
        m_ref[...] = m0
        l_ref[...] = jnp.sum(p, axis=1, keepdims=True)
        acc_ref[...] = _dot(p, jnp.concatenate([vn_ref[...], fill], axis=0))

    slp = slp_ref[...]
    qbd = qbd_ref[...]
    sel_b = sel_ref[...]
    kpos = lax.broadcasted_iota(jnp.int32, (rows, MOBA_BLOCK), 1).astype(F32)
    base = -slp * (tq - kpos)
    m, lsum, acc = m_ref[...], l_ref[...], acc_ref[...]
    for bi in range(bps):
        j = jj * bps + bi
        pages = range(PAGES_PER_BLOCK * bi, PAGES_PER_BLOCK * (bi + 1))
        kmat = jnp.concatenate([_head_major(k_refs[pg], n_heads) for pg in pages], axis=0).astype(BF16)
        vmat = jnp.concatenate([_head_major(v_refs[pg], n_heads) for pg in pages], axis=0).astype(BF16)
        s = _dot_nt(kmat, qbd).T + base
        chosen = _pick_column(sel_b, j)[:, 0:1]
        shift = slp * (past_len - j * MOBA_BLOCK).astype(F32)
        m_new = jnp.maximum(m, jnp.where(chosen, jnp.max(s, axis=1, keepdims=True) - shift, NEG_INF))
        p = jnp.exp(s - jnp.where(chosen, m_new + shift, -NEG_INF))
        corr = jnp.exp(m - m_new)
        lsum = corr * lsum + jnp.sum(p, axis=1, keepdims=True)
        acc = corr * acc + _dot(p, vmat)
        m = m_new
    m_ref[...], l_ref[...], acc_ref[...] = m, lsum, acc

    @pl.when(jj == pl.num_programs(1) - 1)
    def _():
        out = acc / lsum
        o_ref[...] = jnp.concatenate(
            [out[h * t:(h + 1) * t, h * HEAD_DIM:(h + 1) * HEAD_DIM] for h in range(n_heads)], axis=1)


def _moba_sample(z, kmeans, slopes, cache_k, cache_v, page_table, l, q_col, k_col, v_col, n_heads, *, bps):
    b, t, _ = z.shape
    n_pages = page_table.shape[1]
    n_blk = n_pages // PAGES_PER_BLOCK
    width = n_heads * HEAD_DIM
    npg = PAGES_PER_BLOCK * bps
    assert n_blk % bps == 0 and n_blk <= LANE and n_heads * t <= LANE
    assert q_col % width == 0 and k_col % width == 0 and v_col % width == 0

    def page(k):
        return _page_spec(n_heads, lambda bi, jj, pt, sl: (l, pt[bi, npg * jj + k], 0, 0))

    tok = lambda col: pl.BlockSpec((None, t, width), lambda bi, jj, pt, sl: (bi, 0, col // width))
    return pl.pallas_call(
        functools.partial(_moba_sample_kernel, n_heads=n_heads, n_blk=n_blk, bps=bps,
                          past_len=n_pages * PAGE_SIZE, scale=HEAD_DIM ** -0.5),
        grid_spec=pltpu.PrefetchScalarGridSpec(
            num_scalar_prefetch=2,
            grid=(b, n_blk // bps),
            in_specs=[tok(q_col), tok(k_col), tok(v_col),
                      pl.BlockSpec((None, None, n_blk * n_heads, HEAD_DIM), lambda bi, jj, pt, sl: (l, bi, 0, 0))]
            + [page(k) for k in range(npg)] + [page(k) for k in range(npg)],
            out_specs=pl.BlockSpec((None, t, width), lambda bi, jj, pt, sl: (bi, 0, 0)),
            scratch_shapes=[pltpu.VMEM((LANE, width), BF16), pltpu.VMEM((LANE, LANE), BF16),
                            pltpu.VMEM((LANE, 1), F32), pltpu.VMEM((LANE, 1), F32), pltpu.VMEM((LANE, 1), F32),
                            pltpu.VMEM((LANE, width), F32)],
        ),
        out_shape=jax.ShapeDtypeStruct((b, t, width), F32),
        compiler_params=_params(("parallel", "arbitrary")),
        name="moba_sample",
    )(page_table, slopes, z, z, z, kmeans, *([cache_k] * npg), *([cache_v] * npg))


def _merge_ln_kernel(a_ref, s_ref, c_ref, g0_ref, g1_ref, g2_ref, bg0_ref, bg1_ref, bg2_ref,
                     wp_ref, ws_ref, wa_ref, wo_ref, h_ref, g_ref, b_ref, o_ref, *, alpha):
    j = pl.program_id(1)

    @pl.when(j == 0)
    def _():
        o_ref[...] = jnp.zeros_like(o_ref)

    merged = (_sigmoid(g0_ref[...] + bg0_ref[...]) * _dot(a_ref[...], wp_ref[...])
              + _sigmoid(g1_ref[...] + bg1_ref[...]) * _dot(s_ref[...], ws_ref[...])
              + _sigmoid(g2_ref[...] + bg2_ref[...]) * _dot(c_ref[...], wa_ref[...]))
    o_ref[...] += _dot(merged, wo_ref[...])

    @pl.when(j == pl.num_programs(1) - 1)
    def _():
        o_ref[...] = _layer_norm(alpha * h_ref[...] + o_ref[...], g_ref[...], b_ref[...])


def _merge_ln(a_out, s_out, c_out, z, gate_col, b_gate, wp, ws, wa, wo, h, ln_g, ln_b, l, ln_idx,
              *, alpha, tm, tn):
    m, d = h.shape
    gc = gate_col // tn
    nj = d // tn
    row = lambda w: pl.BlockSpec((tm, w), lambda i, j: (i, 0), pipeline_mode=pl.Buffered(1))
    gate = lambda g: pl.BlockSpec((tm, tn), lambda i, j: (i, gc + g * nj + j))
    bias = lambda g: pl.BlockSpec((None, 1, tn), lambda i, j: (l, 0, g * nj + j))
    wcol = lambda k: pl.BlockSpec((None, k, tn), lambda i, j: (l, 0, j))
    return pl.pallas_call(
        functools.partial(_merge_ln_kernel, alpha=alpha),
        grid=(m // tm, nj),
        in_specs=[
            row(a_out.shape[1]), row(s_out.shape[1]), row(c_out.shape[1]),
            gate(0), gate(1), gate(2), bias(0), bias(1), bias(2),
            wcol(a_out.shape[1]), wcol(s_out.shape[1]), wcol(c_out.shape[1]),
            pl.BlockSpec((None, tn, d), lambda i, j: (l, j, 0)),
            row(d),
            pl.BlockSpec((None, 1, d), lambda i, j: (ln_idx, 0, 0)),
            pl.BlockSpec((None, 1, d), lambda i, j: (ln_idx, 0, 0)),
        ],
        out_specs=pl.BlockSpec((tm, d), lambda i, j: (i, 0), pipeline_mode=pl.Buffered(1)),
        out_shape=jax.ShapeDtypeStruct((m, d), F32),
        compiler_params=_params(("parallel", "arbitrary")),
        name="merge_ln",
    )(a_out, s_out, c_out, z, z, z, b_gate, b_gate, b_gate, wp, ws, wa, wo, h, ln_g, ln_b)


def _memattn_ln_kernel(h_ref, wq_ref, mk_ref, mv_ref, wo_ref, g_ref, b_ref, o_ref, *, alpha, scale):
    hx = h_ref[...]
    q = _dot(hx, wq_ref[...])
    mk, mv = mk_ref[...], mv_ref[...]
    outs = []
    for hd in range(N_MEM_HEADS):
        cols = slice(hd * MEM_HEAD_DIM, (hd + 1) * MEM_HEAD_DIM)
        s = _dot_nt(q[:, cols], mk[:, cols]) * scale
        s = s - jnp.max(s, axis=1, keepdims=True)
        p = jnp.exp(s)
        p = p / jnp.sum(p, axis=1, keepdims=True)
        outs.append(_dot(p, mv[:, cols]))
    o = jnp.concatenate(outs, axis=1)
    o_ref[...] = _layer_norm(alpha * hx + _dot(o, wo_ref[...]), g_ref[...], b_ref[...])


def _memattn_ln(h, mem_k, mem_v, mem_l, wq, wo, ln_g, ln_b, l, ln_idx, *, alpha, tm):
    b, t, d = h.shape
    mlen, w = mem_k.shape[2:]
    mem_spec = pl.BlockSpec((None, None, mlen, w), lambda bi, ti: (mem_l, bi, 0, 0))
    return pl.pallas_call(
        functools.partial(_memattn_ln_kernel, alpha=alpha, scale=MEM_HEAD_DIM ** -0.5),
        grid=(b, t // tm),
        in_specs=[
            pl.BlockSpec((None, tm, d), lambda bi, ti: (bi, ti, 0)),
            pl.BlockSpec((None, d, w), lambda bi, ti: (l, 0, 0)),
            mem_spec, mem_spec,
            pl.BlockSpec((None, w, d), lambda bi, ti: (l, 0, 0)),
            pl.BlockSpec((None, 1, d), lambda bi, ti: (ln_idx, 0, 0)),
            pl.BlockSpec((None, 1, d), lambda bi, ti: (ln_idx, 0, 0)),
        ],
        out_specs=pl.BlockSpec((None, tm, d), lambda bi, ti: (bi, ti, 0)),
        out_shape=jax.ShapeDtypeStruct((b, t, d), F32),
        compiler_params=_params(("parallel", "parallel")),
        name="memattn_ln",
    )(h, wq, mem_k, mem_v, wo, ln_g, ln_b)


def kernel(x_prompt, x_sample, cache_k, cache_v, cache_mem_k, cache_mem_v, state_pool, state_ssm_re, state_ssm_im, page_table, mem_prompt, ln_g, ln_b, ffn1_gate, ffn1_up, ffn1_down, w_in, b_gate, pool_w, pool_scale, ssm_lam_re, ssm_lam_im, ssm_log_dt, ssm_b_re, ssm_b_im, ssm_c_re, ssm_c_im, ssm_d, ssm_glu_w, ssm_glu_b, w_proj_pool, w_proj_ssm, w_proj_attn, w_out, mem_wq, mem_wk, mem_wv, mem_wo, ffn2_gate, ffn2_up, ffn2_down):
    depth = ln_g.shape[0]
    bp, tp, d = x_prompt.shape
    bs, ts, _ = x_sample.shape
    pool_width = state_pool.shape[3]
    n_groups, n_state = state_ssm_re.shape[2:]
    ssm_width = n_groups * SSM_GROUP
    n_heads = cache_k.shape[3]
    attn_width = n_heads * HEAD_DIM
    n_pages = page_table.shape[1]
    past_len = n_pages * PAGE_SIZE
    n_past_blk = n_pages // PAGES_PER_BLOCK
    mem_len = mem_prompt.shape[1]
    mem_width = N_MEM_HEADS * MEM_HEAD_DIM
    alpha = (2 * depth) ** 0.25
    ssm_col = pool_width
    q_col = ssm_col + ssm_width
    k_col = q_col + attn_width
    v_col = k_col + attn_width
    gate_col = v_col + attn_width

    ln_g2 = ln_g.reshape(depth * 4, 1, d)
    ln_b2 = ln_b.reshape(depth * 4, 1, d)
    b_gate2 = b_gate.reshape(depth, 1, N_BRANCH * d)
    pool_scale2 = pool_scale.reshape(depth, 1, pool_width)
    n_chunk = 4
    lam_re2 = ssm_lam_re.reshape(depth, 1, n_groups * n_state)
    lam_im2 = ssm_lam_im.reshape(depth, 1, n_groups * n_state)
    ldt_rep = jnp.repeat(ssm_log_dt, n_state, axis=1).reshape(depth, 1, n_groups * n_state)
    b_re_c = _block_diag_chunks(jnp.swapaxes(ssm_b_re, 2, 3), n_chunk)
    b_im_c = _block_diag_chunks(jnp.swapaxes(ssm_b_im, 2, 3), n_chunk)
    c_re_c = _block_diag_chunks(jnp.swapaxes(ssm_c_re, 2, 3), n_chunk)
    c_im_c = _block_diag_chunks(jnp.swapaxes(ssm_c_im, 2, 3), n_chunk)
    ssm_d2 = ssm_d.reshape(depth, 1, ssm_width)
    glu_b2 = ssm_glu_b.reshape(depth, 1, ssm_width)
    slopes = jnp.exp2(-8.0 * jnp.arange(1, n_heads + 1, dtype=F32) / n_heads)
    cache_k4 = cache_k.reshape(cache_k.shape[0], cache_k.shape[1], PAGE_SIZE * n_heads, HEAD_DIM)
    cache_v4 = cache_v.reshape(cache_v.shape[0], cache_v.shape[1], PAGE_SIZE * n_heads, HEAD_DIM)
    cmem_k = cache_mem_k.reshape(depth, bs, mem_len, mem_width)
    cmem_v = cache_mem_v.reshape(depth, bs, mem_len, mem_width)
    mem_rows = mem_prompt.reshape(bp * mem_len, d)
    zero_pool = jnp.zeros((bp, POOL_HALO, pool_width), F32)
    zero_state = jnp.zeros((bp, 1, n_groups * n_state), F32)
    state_pool16 = jnp.pad(state_pool, ((0, 0), (0, 0), (POOL_HALO - POOL_BUF, 0), (0, 0)))
    s_re_in = state_ssm_re.reshape(depth, bs, 1, n_groups * n_state)
    s_im_in = state_ssm_im.reshape(depth, bs, 1, n_groups * n_state)

    kmeans = _paged_block_means(cache_k4, page_table, n_heads, bps=4)

    hp = x_prompt.reshape(bp * tp, d)
    hs = x_sample.reshape(bs * ts, d)
    mp, ms = bp * tp, bs * ts
    outs = {k: [] for k in ("kp", "vp", "poolp", "rep", "imp", "mkp", "mvp", "ks", "vs", "pools", "res", "ims")}

    def mix_common(h, z3, a_out, s_out, c_out, l, tm):
        m = h.shape[0]
        return _merge_ln(a_out.reshape(m, -1), s_out.reshape(m, -1), c_out.reshape(m, -1),
                         z3.reshape(m, -1), gate_col, b_gate2, w_proj_pool, w_proj_ssm, w_proj_attn, w_out,
                         h, ln_g2, ln_b2, l, 4 * l + 1, alpha=alpha, tm=tm, tn=256)

    for l in range(depth):
        mk_p = _matmul(mem_rows, mem_wk, l, tm=bp * mem_len, tn=mem_width)
        mv_p = _matmul(mem_rows, mem_wv, l, tm=bp * mem_len, tn=mem_width)
        hp = _ffn_ln(hp, ffn1_gate, ffn1_up, ffn1_down, ln_g2, ln_b2, l, 4 * l, alpha=alpha, tm=1024, tf=256)
        zp = _matmul(hp, w_in, l, tm=2048, tn=512).reshape(bp, tp, -1)
        a_out = _pool(zp, zero_pool, pool_w, pool_scale2, l, tl=256, start=0)
        s_out, re_p, im_p = _s5(zp, ssm_col, zero_state, zero_state, lam_re2, lam_im2, ldt_rep,
                                b_re_c, b_im_c, c_re_c, c_im_c, ssm_d2, ssm_glu_w, glu_b2, l, tl=256)
        c_out = _moba_prompt(zp, slopes, q_col, k_col, v_col, n_heads, hp=4, cb=2)
        hp = mix_common(hp, zp, a_out, s_out, c_out, l, 1024)
        hp = _memattn_ln(hp.reshape(bp, tp, d), mk_p.reshape(1, bp, mem_len, mem_width),
                         mv_p.reshape(1, bp, mem_len, mem_width), 0, mem_wq, mem_wo, ln_g2, ln_b2, l, 4 * l + 2,
                         alpha=alpha, tm=512).reshape(mp, d)
        hp = _ffn_ln(hp, ffn2_gate, ffn2_up, ffn2_down, ln_g2, ln_b2, l, 4 * l + 3, alpha=alpha, tm=1024, tf=256)
        outs["kp"].append(zp[:, :, k_col:v_col].reshape(bp, tp, n_heads, HEAD_DIM))
        outs["vp"].append(zp[:, :, v_col:gate_col].reshape(bp, tp, n_heads, HEAD_DIM))
        outs["poolp"].append(zp[:, tp - POOL_BUF:, :pool_width])
        outs["rep"].append(re_p.reshape(bp, n_groups, n_state))
        outs["imp"].append(im_p.reshape(bp, n_groups, n_state))
        outs["mkp"].append(mk_p.reshape(bp, mem_len, N_MEM_HEADS, MEM_HEAD_DIM))
        outs["mvp"].append(mv_p.reshape(bp, mem_len, N_MEM_HEADS, MEM_HEAD_DIM))

        hs = _ffn_ln(hs, ffn1_gate, ffn1_up, ffn1_down, ln_g2, ln_b2, l, 4 * l, alpha=alpha, tm=ms, tf=512)
        zs = _matmul(hs, w_in, l, tm=ms, tn=512).reshape(bs, ts, -1)
        a_out = _pool(zs, state_pool16[l], pool_w, pool_scale2, l, tl=ts, start=past_len)
        s_out, re_s, im_s = _s5(zs, ssm_col, s_re_in[l], s_im_in[l], lam_re2, lam_im2, ldt_rep,
                                b_re_c, b_im_c, c_re_c, c_im_c, ssm_d2, ssm_glu_w, glu_b2, l, tl=ts)
        c_out = _moba_sample(zs, kmeans, slopes, cache_k4, cache_v4, page_table, l, q_col, k_col, v_col,
                             n_heads, bps=2)
        hs = mix_common(hs, zs, a_out, s_out, c_out, l, ms)
        hs = _memattn_ln(hs.reshape(bs, ts, d), cmem_k, cmem_v, l, mem_wq, mem_wo, ln_g2, ln_b2, l, 4 * l + 2,
                         alpha=alpha, tm=ts).reshape(ms, d)
        hs = _ffn_ln(hs, ffn2_gate, ffn2_up, ffn2_down, ln_g2, ln_b2, l, 4 * l + 3, alpha=alpha, tm=ms, tf=512)
        outs["ks"].append(zs[:, :, k_col:v_col].reshape(bs, ts, n_heads, HEAD_DIM))
        outs["vs"].append(zs[:, :, v_col:gate_col].reshape(bs, ts, n_heads, HEAD_DIM))
        outs["pools"].append(jnp.concatenate([state_pool[l], zs[:, :, :pool_width]], axis=1)[:, -POOL_BUF:])
        outs["res"].append(re_s.reshape(bs, n_groups, n_state))
        outs["ims"].append(im_s.reshape(bs, n_groups, n_state))

    st = lambda k: jnp.stack(outs[k])
    return (hp.reshape(bp, tp, d), hs.reshape(bs, ts, d),
            st("kp"), st("vp"), st("poolp"), st("rep"), st("imp"), st("mkp"), st("mvp"),
            st("ks"), st("vs"), st("pools"), st("res"), st("ims"))
```

```python
import functools
import math

import jax
import jax.numpy as jnp
from jax import lax
from jax.experimental import pallas as pl
from jax.experimental.pallas import tpu as pltpu

F32 = jnp.float32
BF16 = jnp.bfloat16

POOL_WINDOWS = (2, 4, 8, 16)
POOL_BUF = max(POOL_WINDOWS) - 1
POOL_HALO = 16
SSM_GROUP = 16
SSM_STATE = 64
HEAD_DIM = 128
MOBA_BLOCK = 256
MOBA_TOPK = 3
PAGE_SIZE = 128
PAGES_PER_BLOCK = MOBA_BLOCK // PAGE_SIZE
N_MEM_HEADS = 4
MEM_HEAD_DIM = 128
N_BRANCH = 3
LN_EPS = 1e-5
NEG_INF = -1e30
LANE = 128
SCAN_COLS = 2
VMEM_LIMIT = 56 * 1024 * 1024


def _params(sem, vmem=VMEM_LIMIT):
    return pltpu.CompilerParams(dimension_semantics=sem, vmem_limit_bytes=vmem)


def _dot(a, b):
    return jnp.dot(a.astype(BF16), b.astype(BF16), preferred_element_type=F32)


def _dot_nt(a, b):
    return lax.dot_general(a.astype(BF16), b.astype(BF16), (((1,), (1,)), ((), ())),
                           preferred_element_type=F32)


def _split(a):
    hi = a.astype(BF16)
    lo = (a - hi.astype(F32)).astype(BF16)
    return hi, lo


def _dot3(a, b):
    ah, al = _split(a)
    bh, bl = _split(b)
    d = functools.partial(jnp.dot, preferred_element_type=F32)
    return d(ah, bh) + (d(ah, bl) + d(al, bh))


def _dot3_nt(a, b):
    ah, al = _split(a)
    bh, bl = _split(b)
    d = functools.partial(lax.dot_general, dimension_numbers=(((1,), (1,)), ((), ())),
                          preferred_element_type=F32)
    return d(ah, bh) + (d(ah, bl) + d(al, bh))


def _layer_norm(y, g, b):
    mu = jnp.mean(y, axis=-1, keepdims=True)
    d = y - mu
    var = jnp.mean(d * d, axis=-1, keepdims=True)
    return d * lax.rsqrt(var + LN_EPS) * g + b


def _sigmoid(x):
    return 0.5 * jnp.tanh(0.5 * x) + 0.5


def _ffn_ln_kernel(x_ref, wg_ref, wu_ref, wd_ref, g_ref, b_ref, o_ref, xb_ref, *, alpha):
    f = pl.program_id(1)

    @pl.when(f == 0)
    def _():
        xb_ref[...] = x_ref[...].astype(BF16)
        o_ref[...] = jnp.zeros_like(o_ref)

    xb = xb_ref[...]
    gate = jnp.dot(xb, wg_ref[...].astype(BF16), preferred_element_type=F32)
    up = jnp.dot(xb, wu_ref[...].astype(BF16), preferred_element_type=F32)
    hid = gate * _sigmoid(gate) * up
    o_ref[...] += _dot(hid, wd_ref[...])

    @pl.when(f == pl.num_programs(1) - 1)
    def _():
        y = alpha * x_ref[...] + 0.5 * o_ref[...]
        o_ref[...] = _layer_norm(y, g_ref[...], b_ref[...])


def _ffn_ln(x, wg, wu, wd, ln_g, ln_b, l, ln_idx, *, alpha, tm, tf):
    m, d = x.shape
    f = wg.shape[2]
    return pl.pallas_call(
        functools.partial(_ffn_ln_kernel, alpha=alpha),
        grid=(m // tm, f // tf),
        in_specs=[
            pl.BlockSpec((tm, d), lambda i, j: (i, 0), pipeline_mode=pl.Buffered(1)),
            pl.BlockSpec((None, d, tf), lambda i, j: (l, 0, j)),
            pl.BlockSpec((None, d, tf), lambda i, j: (l, 0, j)),
            pl.BlockSpec((None, tf, d), lambda i, j: (l, j, 0)),
            pl.BlockSpec((None, 1, d), lambda i, j: (ln_idx, 0, 0)),
            pl.BlockSpec((None, 1, d), lambda i, j: (ln_idx, 0, 0)),
        ],
        out_specs=pl.BlockSpec((tm, d), lambda i, j: (i, 0), pipeline_mode=pl.Buffered(1)),
        out_shape=jax.ShapeDtypeStruct((m, d), F32),
        scratch_shapes=[pltpu.VMEM((tm, d), BF16)],
        compiler_params=_params(("parallel", "arbitrary")),
        name="ffn_ln",
    )(x, wg, wu, wd, ln_g, ln_b)


def _matmul_kernel(x_ref, w_ref, o_ref, xb_ref):
    @pl.when(pl.program_id(1) == 0)
    def _():
        xb_ref[...] = x_ref[...].astype(BF16)

    o_ref[...] = jnp.dot(xb_ref[...], w_ref[...].astype(BF16), preferred_element_type=F32)


def _matmul(x, w, l, *, tm, tn):
    m, k = x.shape
    n = w.shape[2]
    return pl.pallas_call(
        _matmul_kernel,
        grid=(m // tm, n // tn),
        in_specs=[
            pl.BlockSpec((tm, k), lambda i, j: (i, 0), pipeline_mode=pl.Buffered(1)),
            pl.BlockSpec((None, k, tn), lambda i, j: (l, 0, j)),
        ],
        out_specs=pl.BlockSpec((tm, tn), lambda i, j: (i, j)),
        out_shape=jax.ShapeDtypeStruct((m, n), F32),
        scratch_shapes=[pltpu.VMEM((tm, k), BF16)],
        compiler_params=_params(("parallel", "arbitrary")),
        name="matmul",
    )(x, w)


def _pool_kernel(u_ref, buf_ref, w_ref, scale_ref, o_ref, xx_ref, *, tl, start):
    ti = pl.program_id(1)

    @pl.when(ti == 0)
    def _():
        xx_ref[0:POOL_HALO, :] = buf_ref[...]

    u = u_ref[...]
    xx_ref[POOL_HALO:POOL_HALO + tl, :] = u
    pos = start + ti * tl + lax.broadcasted_iota(jnp.int32, (tl, 1), 0)
    for gi, w in enumerate(POOL_WINDOWS):
        cols = slice(gi * LANE, (gi + 1) * LANE)
        win = xx_ref[POOL_HALO:POOL_HALO + tl, cols]
        for i in range(1, w):
            win = win + xx_ref[POOL_HALO - i:POOL_HALO - i + tl, cols]
        cnt = jnp.minimum(pos + 1, w).astype(F32)
        pooled = win / cnt - u[:, cols]
        o_ref[:, cols] = (_dot3(pooled, w_ref[gi]) * scale_ref[:, cols]).astype(o_ref.dtype)
    xx_ref[0:POOL_HALO, :] = xx_ref[tl:tl + POOL_HALO, :]


def _pool(z, buf16, pool_w, pool_scale, l, *, tl, start, out_dtype):
    b, t, _ = z.shape
    w = buf16.shape[2]
    assert w == len(POOL_WINDOWS) * LANE
    return pl.pallas_call(
        functools.partial(_pool_kernel, tl=tl, start=start),
        grid=(b, t // tl),
        in_specs=[
            pl.BlockSpec((None, tl, w), lambda bi, ti: (bi, ti, 0)),
            pl.BlockSpec((None, POOL_HALO, w), lambda bi, ti: (bi, 0, 0)),
            pl.BlockSpec((None, len(POOL_WINDOWS), LANE, LANE), lambda bi, ti: (l, 0, 0, 0)),
            pl.BlockSpec((None, 1, w), lambda bi, ti: (l, 0, 0)),
        ],
        out_specs=pl.BlockSpec((None, tl, w), lambda bi, ti: (bi, ti, 0)),
        out_shape=jax.ShapeDtypeStruct((b, t, w), out_dtype),
        scratch_shapes=[pltpu.VMEM((POOL_HALO + tl, w), F32)],
        compiler_params=_params(("parallel", "arbitrary")),
        name="pool",
    )(z, buf16, pool_w, pool_scale)


def _s5_kernel(u_ref, s0r_ref, s0i_ref, lr_ref, li_ref, ldt_ref, br_ref, bi_ref, cr_ref, ci_ref,
               d_ref, wg_ref, bg_ref, y_ref, sr_ref, si_ref,
               pr_ref, pi_ref, qr_ref, qi_ref, apr_ref, api_ref, zr_ref, zi_ref, car_ref, cai_ref,
               *, tl, pad, n_chunk):
    ti = pl.program_id(1)
    n_state = lr_ref.shape[1]
    n_pass = int(math.log2(tl))
    uc = u_ref.shape[1] // n_chunk
    sc = n_state // n_chunk

    @pl.when(ti == 0)
    def _():
        dt = jnp.exp(ldt_ref[...])
        lr, li = lr_ref[...], li_ref[...]
        mag = jnp.exp(lr * dt)
        a_re, a_im = mag * jnp.cos(li * dt), mag * jnp.sin(li * dt)
        den = lr * lr + li * li
        zr_ref[...] = ((a_re - 1.0) * lr + a_im * li) / den
        zi_ref[...] = (a_im * lr - (a_re - 1.0) * li) / den
        pw_re, pw_im = a_re, a_im
        for j in range(n_pass):
            apr_ref[j:j + 1, :] = pw_re
            api_ref[j:j + 1, :] = pw_im
            pw_re, pw_im = pw_re * pw_re - pw_im * pw_im, 2.0 * (pw_re * pw_im)
        car_ref[...] = s0r_ref[...]
        cai_ref[...] = s0i_ref[...]
        zeros = jnp.zeros((pad, n_state), F32)
        pr_ref[0:pad, :] = zeros
        pi_ref[0:pad, :] = zeros
        qr_ref[0:pad, :] = zeros
        qi_ref[0:pad, :] = zeros

    u = u_ref[...]
    for c in range(n_chunk):
        uu = u[:, c * uc:(c + 1) * uc]
        ub_re = _dot(uu, br_ref[c])
        ub_im = _dot(uu, bi_ref[c])
        cols = slice(c * sc, (c + 1) * sc)
        z_re, z_im = zr_ref[:, cols], zi_ref[:, cols]
        pr_ref[pad:pad + tl, cols] = z_re * ub_re - z_im * ub_im
        pi_ref[pad:pad + tl, cols] = z_re * ub_im + z_im * ub_re
    a_re, a_im = apr_ref[0:1, :], api_ref[0:1, :]
    c_re, c_im = car_ref[...], cai_ref[...]
    pr_ref[pad:pad + 1, :] += a_re * c_re - a_im * c_im
    pi_ref[pad:pad + 1, :] += a_re * c_im + a_im * c_re

    bufs = ((pr_ref, pi_ref), (qr_ref, qi_ref))
    for j in range(n_pass):
        k = 1 << j
        (s_re, s_im), (d_re, d_im) = bufs[j % 2], bufs[(j + 1) % 2]

        def col_body(cb, carry, k=k, j=j, s_re=s_re, s_im=s_im, d_re=d_re, d_im=d_im):
            for sub in range(SCAN_COLS):
                cols = pl.ds(pl.multiple_of((cb * SCAN_COLS + sub) * LANE, LANE), LANE)
                ar, ai = apr_ref[j:j + 1, cols], api_ref[j:j + 1, cols]
                cur_re, cur_im = s_re[pad:pad + tl, cols], s_im[pad:pad + tl, cols]
                sh_re, sh_im = s_re[pad - k:pad - k + tl, cols], s_im[pad - k:pad - k + tl, cols]
                d_re[pad:pad + tl, cols] = cur_re + (ar * sh_re - ai * sh_im)
                d_im[pad:pad + tl, cols] = cur_im + (ar * sh_im + ai * sh_re)
            return carry

        lax.fori_loop(0, n_state // (SCAN_COLS * LANE), col_body, 0)
    f_re, f_im = bufs[n_pass % 2]

    car_ref[...] = f_re[pad + tl - 1:pad + tl, :]
    cai_ref[...] = f_im[pad + tl - 1:pad + tl, :]
    sr_ref[...] = car_ref[...]
    si_ref[...] = cai_ref[...]

    ys = []
    for c in range(n_chunk):
        cols = slice(c * sc, (c + 1) * sc)
        ys.append(_dot(f_re[pad:pad + tl, cols], cr_ref[c]) - _dot(f_im[pad:pad + tl, cols], ci_ref[c]))
    y = jnp.concatenate(ys, axis=1) + d_ref[...] * u
    y = 0.5 * y * (1.0 + jnp.tanh(math.sqrt(2.0 / math.pi) * (y + 0.044715 * (y * y * y))))
    y_ref[...] = (y * _sigmoid(_dot(y, wg_ref[...]) + bg_ref[...])).astype(y_ref.dtype)


def _s5(z, col0, s0_re, s0_im, lam_re, lam_im, ldt_rep, b_re_c, b_im_c, c_re_c, c_im_c, ssm_d, glu_w, glu_b, l,
        *, tl, out_dtype):
    b, t, _ = z.shape
    n_state = s0_re.shape[2]
    n_chunk, uc, sc = b_re_c.shape[1:]
    u_w = n_chunk * uc
    assert col0 % u_w == 0 and n_chunk * sc == n_state
    pad = max(8, tl // 2)
    n_pass = int(math.log2(tl))
    assert 1 << n_pass == tl
    vec = lambda w: pl.BlockSpec((None, 1, w), lambda bi, ti: (l, 0, 0))
    state_spec = pl.BlockSpec((None, 1, n_state), lambda bi, ti: (bi, 0, 0))
    return pl.pallas_call(
        functools.partial(_s5_kernel, tl=tl, pad=pad, n_chunk=n_chunk),
        grid=(b, t // tl),
        in_specs=[
            pl.BlockSpec((None, tl, u_w), lambda bi, ti: (bi, ti, col0 // u_w)),
            state_spec, state_spec,
            vec(n_state), vec(n_state), vec(n_state),
            pl.BlockSpec((None, n_chunk, uc, sc), lambda bi, ti: (l, 0, 0, 0)),
            pl.BlockSpec((None, n_chunk, uc, sc), lambda bi, ti: (l, 0, 0, 0)),
            pl.BlockSpec((None, n_chunk, sc, uc), lambda bi, ti: (l, 0, 0, 0)),
            pl.BlockSpec((None, n_chunk, sc, uc), lambda bi, ti: (l, 0, 0, 0)),
            vec(u_w),
            pl.BlockSpec((None, u_w, u_w), lambda bi, ti: (l, 0, 0)),
            vec(u_w),
        ],
        out_specs=[
            pl.BlockSpec((None, tl, u_w), lambda bi, ti: (bi, ti, 0)),
            state_spec, state_spec,
        ],
        out_shape=[
            jax.ShapeDtypeStruct((b, t, u_w), out_dtype),
            jax.ShapeDtypeStruct((b, 1, n_state), F32),
            jax.ShapeDtypeStruct((b, 1, n_state), F32),
        ],
        scratch_shapes=[pltpu.VMEM((pad + tl, n_state), F32)] * 4
        + [pltpu.VMEM((max(8, n_pass), n_state), F32)] * 2
        + [pltpu.VMEM((1, n_state), F32)] * 4,
        compiler_params=_params(("parallel", "arbitrary")),
        name="s5",
    )(z, s0_re, s0_im, lam_re, lam_im, ldt_rep, b_re_c, b_im_c, c_re_c, c_im_c, ssm_d, glu_w, glu_b)


def _block_diag_chunks(w, n_chunk):
    nl, g, r, c = w.shape
    gc = g // n_chunk
    w = w.reshape(nl, n_chunk, gc, r, c)
    eye = jnp.eye(gc, dtype=w.dtype)
    out = w[:, :, :, :, None, :] * eye[None, None, :, None, :, None]
    return out.reshape(nl, n_chunk, gc * r, gc * c)


def _top_blocks(gate, n_valid_col):
    lane = lax.broadcasted_iota(jnp.int32, gate.shape, 1)
    lane_f = lane.astype(F32)
    gate = jnp.where(lane < n_valid_col, gate, NEG_INF)
    sel = jnp.zeros(gate.shape, F32)
    for _ in range(MOBA_TOPK):
        m = jnp.max(gate, axis=1, keepdims=True)
        idx = jnp.min(jnp.where(gate == m, lane_f, float(gate.shape[1])), axis=1, keepdims=True)
        hit = lane_f == idx
        valid = jnp.broadcast_to(m, gate.shape) > 0.5 * NEG_INF
        sel = jnp.where(hit & valid, 1.0, sel)
        gate = jnp.where(hit, NEG_INF, gate)
    return sel


def _pick_column(sel_b, j):
    k = lax.broadcasted_iota(jnp.int32, (LANE, LANE), 0)
    onehot = jnp.where(k == j, 1.0, 0.0).astype(BF16)
    return jnp.dot(sel_b, onehot, preferred_element_type=F32) > 0.5


def _lane_tile(x, k):
    return x if k == 1 else jnp.concatenate([x] * k, axis=1)


def _moba_prompt_kernel(slope_ref, q_ref, k_ref, v_ref, o_ref, kb_ref, vb_ref, km_ref, *, n_blk, hp, cb, scale):
    hg = pl.program_id(1)
    n = pl.program_id(2)
    blk = MOBA_BLOCK
    wc = cb * blk
    rows = hp * blk
    t = k_ref.shape[0]
    stack = lambda per_head: jnp.concatenate(per_head, axis=0)
    head_rows = lambda x, hh: x[hh * blk:(hh + 1) * blk]

    @pl.when(n == 0)
    def _():
        km_ref[...] = jnp.zeros_like(km_ref)
        for hh in range(hp):
            cols = slice(hh * HEAD_DIM, (hh + 1) * HEAD_DIM)
            kb_ref[hh] = k_ref[:, cols].astype(BF16)
            vb_ref[hh, :, 0:HEAD_DIM] = v_ref[:, cols].astype(BF16)
            vb_ref[hh, :, HEAD_DIM:] = jnp.ones((t, HEAD_DIM), BF16)
            for j in range(n_blk):
                km_ref[hh, j:j + 1, :] = jnp.mean(k_ref[j * blk:(j + 1) * blk, cols], axis=0, keepdims=True)

    slopes = [slope_ref[hg * hp + hh] for hh in range(hp)]
    cpos = lax.broadcasted_iota(jnp.int32, (1, wc), 1).astype(F32)
    col_bias = [s * cpos for s in slopes]
    slope_rows = stack([jnp.zeros((blk, LANE), F32) + s for s in slopes])
    qs = [q_ref[:, hh * HEAD_DIM:(hh + 1) * HEAD_DIM] for hh in range(hp)]
    qbs = [(q * scale).astype(BF16) for q in qs]
    gate = stack([_dot_nt(q, km_ref[hh]) for hh, q in enumerate(qs)])
    sel_b = _top_blocks(gate, n).astype(BF16)
    rowmax = lambda x: jnp.broadcast_to(jnp.max(x, axis=1, keepdims=True), (rows, LANE))

    own0 = pl.multiple_of(n * blk, blk)
    causal = stack([lax.broadcasted_iota(jnp.int32, (blk, blk), 0)
                    >= lax.broadcasted_iota(jnp.int32, (blk, blk), 1)] * hp)
    s = stack([_dot_nt(qb, kb_ref[hh, pl.ds(own0, blk), :]) + col_bias[hh][:, :blk] for hh, qb in enumerate(qbs)])
    s = jnp.where(causal, s, NEG_INF)
    m0 = rowmax(s)
    p = jnp.exp(s - _lane_tile(m0, blk // LANE))
    acc0 = stack([_dot(head_rows(p, hh), vb_ref[hh, pl.ds(own0, blk), :]) for hh in range(hp)])

    def body(c, carry):
        m, acc = carry
        c0 = pl.multiple_of(c * wc, wc)
        shift = slope_rows * (n * blk - c * wc).astype(F32)
        s = stack([_dot_nt(qb, kb_ref[hh, pl.ds(c0, wc), :]) + col_bias[hh] for hh, qb in enumerate(qbs)])
        groups = [s[:, g * blk:(g + 1) * blk] for g in range(cb)]
        chosen = [_pick_column(sel_b, c * cb + g) for g in range(cb)]
        m_new = m
        for g in range(cb):
            m_new = jnp.maximum(m_new, jnp.where(chosen[g], rowmax(groups[g]) - shift, NEG_INF))
        p = jnp.concatenate(
            [jnp.exp(groups[g] - _lane_tile(jnp.where(chosen[g], m_new + shift, -NEG_INF), blk // LANE))
             for g in range(cb)], axis=1)
        pv = stack([_dot(head_rows(p, hh), vb_ref[hh, pl.ds(c0, wc), :]) for hh in range(hp)])
        return m_new, _lane_tile(jnp.exp(m - m_new), 2 * HEAD_DIM // LANE) * acc + pv

    _, acc = lax.fori_loop(0, (n + cb - 1) // cb, body, (m0, acc0))
    for hh in range(hp):
        a = head_rows(acc, hh)
        o_ref[:, hh * HEAD_DIM:(hh + 1) * HEAD_DIM] = (a[:, :HEAD_DIM] / a[:, HEAD_DIM:]).astype(o_ref.dtype)


def _moba_prompt(z, slopes, q_col, k_col, v_col, n_heads, *, hp, cb):
    b, t, _ = z.shape
    n_blk = t // MOBA_BLOCK
    w = hp * HEAD_DIM
    assert n_blk * MOBA_BLOCK == t and n_blk <= LANE and n_heads % hp == 0 and n_blk % cb == 0
    assert q_col % w == 0 and k_col % w == 0 and v_col % w == 0
    qc, kc, vc = q_col // w, k_col // w, v_col // w
    whole = lambda c: pl.BlockSpec((None, t, w), lambda bi, hg, n, s: (bi, 0, c + hg),
                                   pipeline_mode=pl.Buffered(1))
    return pl.pallas_call(
        functools.partial(_moba_prompt_kernel, n_blk=n_blk, hp=hp, cb=cb, scale=HEAD_DIM ** -0.5),
        grid_spec=pltpu.PrefetchScalarGridSpec(
            num_scalar_prefetch=1,
            grid=(b, n_heads // hp, n_blk),
            in_specs=[
                pl.BlockSpec((None, MOBA_BLOCK, w), lambda bi, hg, n, s: (bi, n, qc + hg)),
                whole(kc), whole(vc),
            ],
            out_specs=pl.BlockSpec((None, MOBA_BLOCK, w), lambda bi, hg, n, s: (bi, n, hg)),
            scratch_shapes=[pltpu.VMEM((hp, t, HEAD_DIM), BF16), pltpu.VMEM((hp, t, 2 * HEAD_DIM), BF16),
                            pltpu.VMEM((hp, LANE, HEAD_DIM), F32)],
        ),
        out_shape=jax.ShapeDtypeStruct((b, t, n_heads * HEAD_DIM), BF16),
        compiler_params=_params(("parallel", "parallel", "arbitrary")),
        name="moba_prompt",
    )(slopes, z, z, z)


def _page_spec(n_heads, index_map):
    return pl.BlockSpec((None, None, PAGE_SIZE * n_heads, HEAD_DIM), index_map)


def _kmean_kernel(pt_ref, *refs, n_heads, bps):
    pages, o_ref = refs[:PAGES_PER_BLOCK * bps], refs[PAGES_PER_BLOCK * bps]
    jj = pl.program_id(2)
    for bi in range(bps):
        tot = jnp.zeros((n_heads, HEAD_DIM), F32)
        for pg in range(PAGES_PER_BLOCK):
            page = pages[PAGES_PER_BLOCK * bi + pg][...].reshape(PAGE_SIZE, n_heads, HEAD_DIM)
            tot = tot + jnp.sum(page, axis=0)
        row0 = pl.multiple_of((jj * bps + bi) * n_heads, n_heads)
        o_ref[pl.ds(row0, n_heads), :] = tot * (1.0 / MOBA_BLOCK)


def _paged_block_means(cache, page_table, n_heads, *, bps):
    nl = cache.shape[0]
    b, n_pages = page_table.shape
    n_blk = n_pages // PAGES_PER_BLOCK
    assert n_blk % bps == 0

    def spec(k):
        return _page_spec(n_heads, lambda l, bi, jj, pt: (l, pt[bi, PAGES_PER_BLOCK * bps * jj + k], 0, 0))

    return pl.pallas_call(
        functools.partial(_kmean_kernel, n_heads=n_heads, bps=bps),
        grid_spec=pltpu.PrefetchScalarGridSpec(
            num_scalar_prefetch=1,
            grid=(nl, b, n_blk // bps),
            in_specs=[spec(k) for k in range(PAGES_PER_BLOCK * bps)],
            out_specs=pl.BlockSpec((None, None, n_blk * n_heads, HEAD_DIM), lambda l, bi, jj, pt: (l, bi, 0, 0)),
        ),
        out_shape=jax.ShapeDtypeStruct((nl, b, n_blk * n_heads, HEAD_DIM), F32),
        compiler_params=_params(("parallel", "parallel", "arbitrary")),
        name="paged_block_means",
    )(page_table, *([cache] * (PAGES_PER_BLOCK * bps)))


def _head_major(page_ref, n_heads):
    return jnp.concatenate([page_ref[pl.ds(h, PAGE_SIZE, stride=n_heads), :] for h in range(n_heads)], axis=1)


def _moba_sample_kernel(pt_ref, slope_ref, q_ref, kn_ref, vn_ref, km_ref, *refs,
                        n_heads, n_blk, bps, past_len, scale):
    npg = PAGES_PER_BLOCK * bps
    k_refs, v_refs = refs[:npg], refs[npg:2 * npg]
    o_ref, qbd_ref, sel_ref, slp_ref, m_ref, l_ref, acc_ref = refs[2 * npg:]
    jj = pl.program_id(1)
    t, width = q_ref.shape
    rows = n_heads * t
    rid = lax.broadcasted_iota(jnp.int32, (rows, 1), 0).astype(F32)
    hq = jnp.floor((rid + 0.5) * (1.0 / t))
    tq = rid - t * hq

    @pl.when(jj == 0)
    def _():
        q = q_ref[...]
        c = lax.broadcasted_iota(jnp.int32, (t, width), 1)
        per_head = [jnp.where((c >= h * HEAD_DIM) & (c < (h + 1) * HEAD_DIM), q, 0.0) for h in range(n_heads)]
        qbd_ref[...] = (jnp.concatenate(per_head, axis=0) * scale).astype(BF16)
        gates = []
        for h in range(n_heads):
            km_h = km_ref[pl.ds(h, n_blk, stride=n_heads), :]
            km_h = jnp.concatenate([km_h, jnp.zeros((LANE - n_blk, HEAD_DIM), F32)], axis=0)
            gates.append(_dot_nt(q[:, h * HEAD_DIM:(h + 1) * HEAD_DIM], km_h))
        sel_ref[...] = _top_blocks(jnp.concatenate(gates, axis=0), n_blk).astype(BF16)
        slp = jnp.zeros((rows, 1), F32)
        for h in range(n_heads):
            slp = jnp.where(hq == h, slope_ref[h], slp)
        slp_ref[...] = slp
        fill = jnp.zeros((LANE - t, width), F32)
        s = _dot_nt(qbd_ref[...], jnp.concatenate([kn_ref[...], fill], axis=0))
        cpos = lax.broadcasted_iota(jnp.int32, (rows, LANE), 1).astype(F32)
        s = jnp.where(cpos <= tq, s - slp * (tq - cpos), NEG_INF)
        m0 = jnp.max(s, axis=1, keepdims=True)
        p = jnp.exp(s - m0)
        m_ref[...] = m0
        l_ref[...] = jnp.sum(p, axis=1, keepdims=True)
        acc_ref[...] = _dot(p, jnp.concatenate([vn_ref[...], fill], axis=0))

    slp = slp_ref[...]
    qbd = qbd_ref[...]
    sel_b = sel_ref[...]
    kmat = jnp.concatenate([_head_major(r, n_heads) for r in k_refs], axis=0).astype(BF16)
    vmat = jnp.concatenate([_head_major(r, n_heads) for r in v_refs], axis=0).astype(BF16)
    kpos = lax.broadcasted_iota(jnp.int32, (rows, bps * MOBA_BLOCK), 1).astype(F32)
    s = _dot_nt(qbd, kmat) - slp * (tq - kpos)
    shift = slp * (past_len - jj * (bps * MOBA_BLOCK)).astype(F32)
    groups = [s[:, g * MOBA_BLOCK:(g + 1) * MOBA_BLOCK] for g in range(bps)]
    chosen = [_pick_column(sel_b, jj * bps + g)[:, 0:1] for g in range(bps)]
    m = m_ref[...]
    m_new = m
    for g in range(bps):
        m_new = jnp.maximum(m_new, jnp.where(chosen[g], jnp.max(groups[g], axis=1, keepdims=True) - shift, NEG_INF))
    p = jnp.concatenate([jnp.exp(groups[g] - jnp.where(chosen[g], m_new + shift, -NEG_INF)) for g in range(bps)],
                        axis=1)
    corr = jnp.exp(m - m_new)
    lsum = corr * l_ref[...] + jnp.sum(p, axis=1, keepdims=True)
    acc = corr * acc_ref[...] + _dot(p, vmat)
    m_ref[...], l_ref[...], acc_ref[...] = m_new, lsum, acc

    @pl.when(jj == pl.num_programs(1) - 1)
    def _():
        out = acc / lsum
        o_ref[...] = jnp.concatenate(
            [out[h * t:(h + 1) * t, h * HEAD_DIM:(h + 1) * HEAD_DIM] for h in range(n_heads)], axis=1)


def _moba_sample(z, kmeans, slopes, cache_k, cache_v, page_table, l, q_col, k_col, v_col, n_heads, *, bps):
    b, t, _ = z.shape
    n_pages = page_table.shape[1]
    n_blk = n_pages // PAGES_PER_BLOCK
    width = n_heads * HEAD_DIM
    npg = PAGES_PER_BLOCK * bps
    rows = n_heads * t
    assert n_blk % bps == 0 and n_blk <= LANE and t <= LANE and rows % 16 == 0
    assert q_col % width == 0 and k_col % width == 0 and v_col % width == 0

    def page(k):
        return _page_spec(n_heads, lambda bi, jj, pt, sl: (l, pt[bi, npg * jj + k], 0, 0))

    tok = lambda col: pl.BlockSpec((None, t, width), lambda bi, jj, pt, sl: (bi, 0, col // width))
    return pl.pallas_call(
        functools.partial(_moba_sample_kernel, n_heads=n_heads, n_blk=n_blk, bps=bps,
                          past_len=n_pages * PAGE_SIZE, scale=HEAD_DIM ** -0.5),
        grid_spec=pltpu.PrefetchScalarGridSpec(
            num_scalar_prefetch=2,
            grid=(b, n_blk // bps),
            in_specs=[tok(q_col), tok(k_col), tok(v_col),
                      pl.BlockSpec((None, None, n_blk * n_heads, HEAD_DIM), lambda bi, jj, pt, sl: (l, bi, 0, 0))]
            + [page(k) for k in range(npg)] + [page(k) for k in range(npg)],
            out_specs=pl.BlockSpec((None, t, width), lambda bi, jj, pt, sl: (bi, 0, 0)),
            scratch_shapes=[pltpu.VMEM((rows, width), BF16), pltpu.VMEM((rows, LANE), BF16),
                            pltpu.VMEM((rows, 1), F32), pltpu.VMEM((rows, 1), F32), pltpu.VMEM((rows, 1), F32),
                            pltpu.VMEM((rows, width), F32)],
        ),
        out_shape=jax.ShapeDtypeStruct((b, t, width), F32),
        compiler_params=_params(("parallel", "arbitrary")),
        name="moba_sample",
    )(page_table, slopes, z, z, z, kmeans, *([cache_k] * npg), *([cache_v] * npg))


def _merge_ln_kernel(a_ref, s_ref, c_ref, g0_ref, g1_ref, g2_ref, bg0_ref, bg1_ref, bg2_ref,
                     wp_ref, ws_ref, wa_ref, wo_ref, h_ref, g_ref, b_ref, o_ref, *, alpha):
    j = pl.program_id(1)

    @pl.when(j == 0)
    def _():
        o_ref[...] = jnp.zeros_like(o_ref)

    gated = lambda g_ref, bg_ref, x_ref, w_ref: (
        (jnp.tanh(0.5 * (g_ref[...] + bg_ref[...])) + 1.0) * _dot(x_ref[...], w_ref[...]))
    merged = 0.5 * (gated(g0_ref, bg0_ref, a_ref, wp_ref) + gated(g1_ref, bg1_ref, s_ref, ws_ref)
                    + gated(g2_ref, bg2_ref, c_ref, wa_ref))
    o_ref[...] += _dot(merged, wo_ref[...])

    @pl.when(j == pl.num_programs(1) - 1)
    def _():
        o_ref[...] = _layer_norm(alpha * h_ref[...] + o_ref[...], g_ref[...], b_ref[...])


def _merge_ln(a_out, s_out, c_out, z, gate_col, b_gate, wp, ws, wa, wo, h, ln_g, ln_b, l, ln_idx,
              *, alpha, tm, tn):
    m, d = h.shape
    gc = gate_col // tn
    nj = d // tn
    row = lambda w: pl.BlockSpec((tm, w), lambda i, j: (i, 0), pipeline_mode=pl.Buffered(1))
    gate = lambda g: pl.BlockSpec((tm, tn), lambda i, j: (i, gc + g * nj + j))
    bias = lambda g: pl.BlockSpec((None, 1, tn), lambda i, j: (l, 0, g * nj + j))
    wcol = lambda k: pl.BlockSpec((None, k, tn), lambda i, j: (l, 0, j))
    return pl.pallas_call(
        functools.partial(_merge_ln_kernel, alpha=alpha),
        grid=(m // tm, nj),
        in_specs=[
            row(a_out.shape[1]), row(s_out.shape[1]), row(c_out.shape[1]),
            gate(0), gate(1), gate(2), bias(0), bias(1), bias(2),
            wcol(a_out.shape[1]), wcol(s_out.shape[1]), wcol(c_out.shape[1]),
            pl.BlockSpec((None, tn, d), lambda i, j: (l, j, 0)),
            row(d),
            pl.BlockSpec((None, 1, d), lambda i, j: (ln_idx, 0, 0)),
            pl.BlockSpec((None, 1, d), lambda i, j: (ln_idx, 0, 0)),
        ],
        out_specs=pl.BlockSpec((tm, d), lambda i, j: (i, 0), pipeline_mode=pl.Buffered(1)),
        out_shape=jax.ShapeDtypeStruct((m, d), F32),
        compiler_params=_params(("parallel", "arbitrary")),
        name="merge_ln",
    )(a_out, s_out, c_out, z, z, z, b_gate, b_gate, b_gate, wp, ws, wa, wo, h, ln_g, ln_b)


def _memattn_ln_kernel(h_ref, wq_ref, mk_ref, mv_ref, wo_ref, g_ref, b_ref, o_ref, *, alpha, scale):
    hx = h_ref[...]
    q = _dot(hx, wq_ref[...])
    mk, mv = mk_ref[...], mv_ref[...]
    outs = []
    for hd in range(N_MEM_HEADS):
        cols = slice(hd * MEM_HEAD_DIM, (hd + 1) * MEM_HEAD_DIM)
        s = _dot_nt(q[:, cols], mk[:, cols]) * scale
        s = s - jnp.max(s, axis=1, keepdims=True)
        p = jnp.exp(s)
        p = p / jnp.sum(p, axis=1, keepdims=True)
        outs.append(_dot(p, mv[:, cols]))
    o = jnp.concatenate(outs, axis=1)
    o_ref[...] = _layer_norm(alpha * hx + _dot(o, wo_ref[...]), g_ref[...], b_ref[...])


def _memattn_ln(h, mem_k, mem_v, mem_l, wq, wo, ln_g, ln_b, l, ln_idx, *, alpha, tm):
    b, t, d = h.shape
    mlen, w = mem_k.shape[2:]
    mem_spec = pl.BlockSpec((None, None, mlen, w), lambda bi, ti: (mem_l, bi, 0, 0))
    return pl.pallas_call(
        functools.partial(_memattn_ln_kernel, alpha=alpha, scale=MEM_HEAD_DIM ** -0.5),
        grid=(b, t // tm),
        in_specs=[
            pl.BlockSpec((None, tm, d), lambda bi, ti: (bi, ti, 0)),
            pl.BlockSpec((None, d, w), lambda bi, ti: (l, 0, 0)),
            mem_spec, mem_spec,
            pl.BlockSpec((None, w, d), lambda bi, ti: (l, 0, 0)),
            pl.BlockSpec((None, 1, d), lambda bi, ti: (ln_idx, 0, 0)),
            pl.BlockSpec((None, 1, d), lambda bi, ti: (ln_idx, 0, 0)),
        ],
        out_specs=pl.BlockSpec((None, tm, d), lambda bi, ti: (bi, ti, 0)),
        out_shape=jax.ShapeDtypeStruct((b, t, d), F32),
        compiler_params=_params(("parallel", "parallel")),
        name="memattn_ln",
    )(h, wq, mem_k, mem_v, wo, ln_g, ln_b)


def kernel(x_prompt, x_sample, cache_k, cache_v, cache_mem_k, cache_mem_v, state_pool, state_ssm_re, state_ssm_im, page_table, mem_prompt, ln_g, ln_b, ffn1_gate, ffn1_up, ffn1_down, w_in, b_gate, pool_w, pool_scale, ssm_lam_re, ssm_lam_im, ssm_log_dt, ssm_b_re, ssm_b_im, ssm_c_re, ssm_c_im, ssm_d, ssm_glu_w, ssm_glu_b, w_proj_pool, w_proj_ssm, w_proj_attn, w_out, mem_wq, mem_wk, mem_wv, mem_wo, ffn2_gate, ffn2_up, ffn2_down):
    depth = ln_g.shape[0]
    bp, tp, d = x_prompt.shape
    bs, ts, _ = x_sample.shape
    pool_width = state_pool.shape[3]
    n_groups, n_state = state_ssm_re.shape[2:]
    ssm_width = n_groups * SSM_GROUP
    n_heads = cache_k.shape[3]
    attn_width = n_heads * HEAD_DIM
    n_pages = page_table.shape[1]
    past_len = n_pages * PAGE_SIZE
    n_past_blk = n_pages // PAGES_PER_BLOCK
    mem_len = mem_prompt.shape[1]
    mem_width = N_MEM_HEADS * MEM_HEAD_DIM
    alpha = (2 * depth) ** 0.25
    ssm_col = pool_width
    q_col = ssm_col + ssm_width
    k_col = q_col + attn_width
    v_col = k_col + attn_width
    gate_col = v_col + attn_width

    ln_g2 = ln_g.reshape(depth * 4, 1, d)
    ln_b2 = ln_b.reshape(depth * 4, 1, d)
    b_gate2 = b_gate.reshape(depth, 1, N_BRANCH * d)
    pool_scale2 = pool_scale.reshape(depth, 1, pool_width)
    n_chunk = 4
    lam_re2 = ssm_lam_re.reshape(depth, 1, n_groups * n_state)
    lam_im2 = ssm_lam_im.reshape(depth, 1, n_groups * n_state)
    ldt_rep = jnp.repeat(ssm_log_dt, n_state, axis=1).reshape(depth, 1, n_groups * n_state)
    b_re_c = _block_diag_chunks(jnp.swapaxes(ssm_b_re, 2, 3), n_chunk)
    b_im_c = _block_diag_chunks(jnp.swapaxes(ssm_b_im, 2, 3), n_chunk)
    c_re_c = _block_diag_chunks(jnp.swapaxes(ssm_c_re, 2, 3), n_chunk)
    c_im_c = _block_diag_chunks(jnp.swapaxes(ssm_c_im, 2, 3), n_chunk)
    ssm_d2 = ssm_d.reshape(depth, 1, ssm_width)
    glu_b2 = ssm_glu_b.reshape(depth, 1, ssm_width)
    slopes = jnp.exp2(-8.0 * jnp.arange(1, n_heads + 1, dtype=F32) / n_heads)
    cache_k4 = cache_k.reshape(cache_k.shape[0], cache_k.shape[1], PAGE_SIZE * n_heads, HEAD_DIM)
    cache_v4 = cache_v.reshape(cache_v.shape[0], cache_v.shape[1], PAGE_SIZE * n_heads, HEAD_DIM)
    cmem_k = cache_mem_k.reshape(depth, bs, mem_len, mem_width)
    cmem_v = cache_mem_v.reshape(depth, bs, mem_len, mem_width)
    mem_rows = mem_prompt.reshape(bp * mem_len, d)
    zero_pool = jnp.zeros((bp, POOL_HALO, pool_width), F32)
    zero_state = jnp.zeros((bp, 1, n_groups * n_state), F32)
    state_pool16 = jnp.pad(state_pool, ((0, 0), (0, 0), (POOL_HALO - POOL_BUF, 0), (0, 0)))
    s_re_in = state_ssm_re.reshape(depth, bs, 1, n_groups * n_state)
    s_im_in = state_ssm_im.reshape(depth, bs, 1, n_groups * n_state)

    merge_w = [w.astype(BF16) for w in (w_proj_pool, w_proj_ssm, w_proj_attn, w_out)]

    kmeans = _paged_block_means(cache_k4, page_table, n_heads, bps=4)

    hp = x_prompt.reshape(bp * tp, d)
    hs = x_sample.reshape(bs * ts, d)
    mp, ms = bp * tp, bs * ts
    outs = {k: [] for k in ("kp", "vp", "poolp", "rep", "imp", "mkp", "mvp", "ks", "vs", "pools", "res", "ims")}

    def mix_common(h, z3, a_out, s_out, c_out, l, tm):
        m = h.shape[0]
        return _merge_ln(a_out.reshape(m, -1), s_out.reshape(m, -1), c_out.reshape(m, -1),
                         z3.reshape(m, -1), gate_col, b_gate2, *merge_w,
                         h, ln_g2, ln_b2, l, 4 * l + 1, alpha=alpha, tm=tm, tn=512)

    for l in range(depth):
        mk_p = _matmul(mem_rows, mem_wk, l, tm=bp * mem_len, tn=mem_width)
        mv_p = _matmul(mem_rows, mem_wv, l, tm=bp * mem_len, tn=mem_width)
        hp = _ffn_ln(hp, ffn1_gate, ffn1_up, ffn1_down, ln_g2, ln_b2, l, 4 * l, alpha=alpha, tm=1024, tf=256)
        zp = _matmul(hp, w_in, l, tm=2048, tn=512).reshape(bp, tp, -1)
        a_out = _pool(zp, zero_pool, pool_w, pool_scale2, l, tl=256, start=0, out_dtype=BF16)
        s_out, re_p, im_p = _s5(zp, ssm_col, zero_state, zero_state, lam_re2, lam_im2, ldt_rep,
                                b_re_c, b_im_c, c_re_c, c_im_c, ssm_d2, ssm_glu_w, glu_b2, l, tl=256,
                                out_dtype=BF16)
        c_out = _moba_prompt(zp, slopes, q_col, k_col, v_col, n_heads, hp=4, cb=2)
        hp = mix_common(hp, zp, a_out, s_out, c_out, l, 1024)
        hp = _memattn_ln(hp.reshape(bp, tp, d), mk_p.reshape(1, bp, mem_len, mem_width),
                         mv_p.reshape(1, bp, mem_len, mem_width), 0, mem_wq, mem_wo, ln_g2, ln_b2, l, 4 * l + 2,
                         alpha=alpha, tm=512).reshape(mp, d)
        hp = _ffn_ln(hp, ffn2_gate, ffn2_up, ffn2_down, ln_g2, ln_b2, l, 4 * l + 3, alpha=alpha, tm=1024, tf=256)
        outs["kp"].append(zp[:, :, k_col:v_col].reshape(bp, tp, n_heads, HEAD_DIM))
        outs["vp"].append(zp[:, :, v_col:gate_col].reshape(bp, tp, n_heads, HEAD_DIM))
        outs["poolp"].append(zp[:, tp - POOL_BUF:, :pool_width])
        outs["rep"].append(re_p.reshape(bp, n_groups, n_state))
        outs["imp"].append(im_p.reshape(bp, n_groups, n_state))
        outs["mkp"].append(mk_p.reshape(bp, mem_len, N_MEM_HEADS, MEM_HEAD_DIM))
        outs["mvp"].append(mv_p.reshape(bp, mem_len, N_MEM_HEADS, MEM_HEAD_DIM))

        hs = _ffn_ln(hs, ffn1_gate, ffn1_up, ffn1_down, ln_g2, ln_b2, l, 4 * l, alpha=alpha, tm=ms, tf=512)
        zs = _matmul(hs, w_in, l, tm=ms, tn=512).reshape(bs, ts, -1)
        a_out = _pool(zs, state_pool16[l], pool_w, pool_scale2, l, tl=ts, start=past_len, out_dtype=F32)
        s_out, re_s, im_s = _s5(zs, ssm_col, s_re_in[l], s_im_in[l], lam_re2, lam_im2, ldt_rep,
                                b_re_c, b_im_c, c_re_c, c_im_c, ssm_d2, ssm_glu_w, glu_b2, l, tl=ts,
                                out_dtype=F32)
        c_out = _moba_sample(zs, kmeans, slopes, cache_k4, cache_v4, page_table, l, q_col, k_col, v_col,
                             n_heads, bps=4)
        hs = mix_common(hs, zs, a_out, s_out, c_out, l, ms)
        hs = _memattn_ln(hs.reshape(bs, ts, d), cmem_k, cmem_v, l, mem_wq, mem_wo, ln_g2, ln_b2, l, 4 * l + 2,
                         alpha=alpha, tm=ts).reshape(ms, d)
        hs = _ffn_ln(hs, ffn2_gate, ffn2_up, ffn2_down, ln_g2, ln_b2, l, 4 * l + 3, alpha=alpha, tm=ms, tf=512)
        outs["ks"].append(zs[:, :, k_col:v_col].reshape(bs, ts, n_heads, HEAD_DIM))
        outs["vs"].append(zs[:, :, v_col:gate_col].reshape(bs, ts, n_heads, HEAD_DIM))
        outs["pools"].append(jnp.concatenate([state_pool[l], zs[:, :, :pool_width]], axis=1)[:, -POOL_BUF:])
        outs["res"].append(re_s.reshape(bs, n_groups, n_state))
        outs["ims"].append(im_s.reshape(bs, n_groups, n_state))

    st = lambda k: jnp.stack(outs[k])
    return (hp.reshape(bp, tp, d), hs.reshape(bs, ts, d),
            st("kp"), st("vp"), st("poolp"), st("rep"), st("imp"), st("mkp"), st("mvp"),
            st("ks"), st("vs"), st("pools"), st("res"), st("ims"))
```

```python
import functools
import math

import jax
import jax.numpy as jnp
from jax import lax
from jax.experimental import pallas as pl
from jax.experimental.pallas import tpu as pltpu

F32 = jnp.float32
BF16 = jnp.bfloat16

POOL_WINDOWS = (2, 4, 8, 16)
POOL_BUF = max(POOL_WINDOWS) - 1
POOL_HALO = 16
SSM_GROUP = 16
SSM_STATE = 64
HEAD_DIM = 128
MOBA_BLOCK = 256
MOBA_TOPK = 3
PAGE_SIZE = 128
PAGES_PER_BLOCK = MOBA_BLOCK // PAGE_SIZE
N_MEM_HEADS = 4
MEM_HEAD_DIM = 128
N_BRANCH = 3
LN_EPS = 1e-5
NEG_INF = -1e30
LANE = 128
SCAN_COLS = 2
VMEM_LIMIT = 56 * 1024 * 1024


def _params(sem, vmem=VMEM_LIMIT):
    return pltpu.CompilerParams(dimension_semantics=sem, vmem_limit_bytes=vmem)


def _dot(a, b):
    return jnp.dot(a.astype(BF16), b.astype(BF16), preferred_element_type=F32)


def _dot_nt(a, b):
    return lax.dot_general(a.astype(BF16), b.astype(BF16), (((1,), (1,)), ((), ())),
                           preferred_element_type=F32)


def _split(a):
    hi = a.astype(BF16)
    lo = (a - hi.astype(F32)).astype(BF16)
    return hi, lo


def _dot3(a, b):
    ah, al = _split(a)
    bh, bl = _split(b)
    d = functools.partial(jnp.dot, preferred_element_type=F32)
    return d(ah, bh) + (d(ah, bl) + d(al, bh))


def _dot3_nt(a, b):
    ah, al = _split(a)
    bh, bl = _split(b)
    d = functools.partial(lax.dot_general, dimension_numbers=(((1,), (1,)), ((), ())),
                          preferred_element_type=F32)
    return d(ah, bh) + (d(ah, bl) + d(al, bh))


def _layer_norm(y, g, b):
    mu = jnp.mean(y, axis=-1, keepdims=True)
    d = y - mu
    var = jnp.mean(d * d, axis=-1, keepdims=True)
    return d * lax.rsqrt(var + LN_EPS) * g + b


def _sigmoid(x):
    return 0.5 * jnp.tanh(0.5 * x) + 0.5


def _ffn_ln_kernel(x_ref, xs_ref, wg_ref, wu_ref, wd_ref, g_ref, b_ref, o_ref, os_ref, xb_ref, xsb_ref, *, alpha):
    i, f = pl.program_id(0), pl.program_id(1)

    def group_step(x_ref, xb_ref, o_ref):
        @pl.when(f == 0)
        def _():
            xb_ref[...] = x_ref[...].astype(BF16)
            o_ref[...] = jnp.zeros_like(o_ref)

        xb = xb_ref[...]
        gate = jnp.dot(xb, wg_ref[...].astype(BF16), preferred_element_type=F32)
        up = jnp.dot(xb, wu_ref[...].astype(BF16), preferred_element_type=F32)
        hid = gate * _sigmoid(gate) * up
        o_ref[...] += _dot(hid, wd_ref[...])

        @pl.when(f == pl.num_programs(1) - 1)
        def _():
            y = alpha * x_ref[...] + 0.5 * o_ref[...]
            o_ref[...] = _layer_norm(y, g_ref[...], b_ref[...])

    group_step(x_ref, xb_ref, o_ref)

    @pl.when(i == 0)
    def _():
        group_step(xs_ref, xsb_ref, os_ref)


def _ffn_ln(x, xs, wg, wu, wd, ln_g, ln_b, l, ln_idx, *, alpha, tm, tf):
    m, d = x.shape
    ms = xs.shape[0]
    f = wg.shape[2]
    return pl.pallas_call(
        functools.partial(_ffn_ln_kernel, alpha=alpha),
        grid=(m // tm, f // tf),
        in_specs=[
            pl.BlockSpec((tm, d), lambda i, j: (i, 0), pipeline_mode=pl.Buffered(1)),
            pl.BlockSpec((ms, d), lambda i, j: (0, 0), pipeline_mode=pl.Buffered(1)),
            pl.BlockSpec((None, d, tf), lambda i, j: (l, 0, j)),
            pl.BlockSpec((None, d, tf), lambda i, j: (l, 0, j)),
            pl.BlockSpec((None, tf, d), lambda i, j: (l, j, 0)),
            pl.BlockSpec((None, 1, d), lambda i, j: (ln_idx, 0, 0)),
            pl.BlockSpec((None, 1, d), lambda i, j: (ln_idx, 0, 0)),
        ],
        out_specs=[
            pl.BlockSpec((tm, d), lambda i, j: (i, 0), pipeline_mode=pl.Buffered(1)),
            pl.BlockSpec((ms, d), lambda i, j: (0, 0), pipeline_mode=pl.Buffered(1)),
        ],
        out_shape=[jax.ShapeDtypeStruct((m, d), F32), jax.ShapeDtypeStruct((ms, d), F32)],
        scratch_shapes=[pltpu.VMEM((tm, d), BF16), pltpu.VMEM((ms, d), BF16)],
        compiler_params=_params(("arbitrary", "arbitrary")),
        name="ffn_ln",
    )(x, xs, wg, wu, wd, ln_g, ln_b)


def _matmul_kernel(x_ref, w_ref, o_ref, xb_ref):
    @pl.when(pl.program_id(1) == 0)
    def _():
        xb_ref[...] = x_ref[...].astype(BF16)

    o_ref[...] = jnp.dot(xb_ref[...], w_ref[...].astype(BF16), preferred_element_type=F32)


def _matmul(x, w, l, *, tm, tn):
    m, k = x.shape
    n = w.shape[2]
    return pl.pallas_call(
        _matmul_kernel,
        grid=(m // tm, n // tn),
        in_specs=[
            pl.BlockSpec((tm, k), lambda i, j: (i, 0), pipeline_mode=pl.Buffered(1)),
            pl.BlockSpec((None, k, tn), lambda i, j: (l, 0, j)),
        ],
        out_specs=pl.BlockSpec((tm, tn), lambda i, j: (i, j)),
        out_shape=jax.ShapeDtypeStruct((m, n), F32),
        scratch_shapes=[pltpu.VMEM((tm, k), BF16)],
        compiler_params=_params(("parallel", "arbitrary")),
        name="matmul",
    )(x, w)


def _in_proj_kernel(*refs, k_lo, v_lo, n_kv, aliased):
    x_ref, w_ref = refs[:2]
    z_ref, ko_ref, vo_ref, xb_ref = refs[4:] if aliased else refs[2:]
    j = pl.program_id(1)

    @pl.when(j == 0)
    def _():
        xb_ref[...] = x_ref[...].astype(BF16)

    res = jnp.dot(xb_ref[...], w_ref[...].astype(BF16), preferred_element_type=F32)
    z_ref[...] = res

    @pl.when((j >= k_lo) & (j < k_lo + n_kv))
    def _():
        ko_ref[...] = res

    @pl.when((j >= v_lo) & (j < v_lo + n_kv))
    def _():
        vo_ref[...] = res


def _in_proj(x, w, l, k_col, v_col, kv_width, stacks, *, tm, tn):
    m, k = x.shape
    nl, _, n = w.shape
    assert k_col % tn == 0 and v_col % tn == 0 and kv_width % tn == 0
    k_lo, v_lo, n_kv = k_col // tn, v_col // tn, kv_width // tn
    aliased = stacks is not None

    def kv_spec(lo):
        return pl.BlockSpec((None, tm, tn), lambda i, j: (l, i, jnp.clip(j - lo, 0, n_kv - 1)),
                            pipeline_mode=pl.Buffered(1))

    stack_shape = jax.ShapeDtypeStruct((nl, m, kv_width), F32)
    return pl.pallas_call(
        functools.partial(_in_proj_kernel, k_lo=k_lo, v_lo=v_lo, n_kv=n_kv, aliased=aliased),
        grid=(m // tm, n // tn),
        in_specs=[
            pl.BlockSpec((tm, k), lambda i, j: (i, 0), pipeline_mode=pl.Buffered(1)),
            pl.BlockSpec((None, k, tn), lambda i, j: (l, 0, j)),
        ] + ([pl.BlockSpec(memory_space=pl.ANY)] * 2 if aliased else []),
        out_specs=[pl.BlockSpec((tm, tn), lambda i, j: (i, j)), kv_spec(k_lo), kv_spec(v_lo)],
        out_shape=[jax.ShapeDtypeStruct((m, n), F32), stack_shape, stack_shape],
        input_output_aliases={2: 1, 3: 2} if aliased else {},
        scratch_shapes=[pltpu.VMEM((tm, k), BF16)],
        compiler_params=_params(("arbitrary", "arbitrary")),
        name="in_proj",
    )(x, w, *(stacks if aliased else ()))


def _pool_kernel(u_ref, buf_ref, w_ref, scale_ref, o_ref, xx_ref, *, tl, start):
    ti = pl.program_id(1)

    @pl.when(ti == 0)
    def _():
        xx_ref[0:POOL_HALO, :] = buf_ref[...]

    u = u_ref[...]
    xx_ref[POOL_HALO:POOL_HALO + tl, :] = u
    pos = start + ti * tl + lax.broadcasted_iota(jnp.int32, (tl, 1), 0)
    for gi, w in enumerate(POOL_WINDOWS):
        cols = slice(gi * LANE, (gi + 1) * LANE)
        win = xx_ref[POOL_HALO:POOL_HALO + tl, cols]
        for i in range(1, w):
            win = win + xx_ref[POOL_HALO - i:POOL_HALO - i + tl, cols]
        cnt = jnp.minimum(pos + 1, w).astype(F32)
        pooled = win / cnt - u[:, cols]
        o_ref[:, cols] = (_dot3(pooled, w_ref[gi]) * scale_ref[:, cols]).astype(o_ref.dtype)
    xx_ref[0:POOL_HALO, :] = xx_ref[tl:tl + POOL_HALO, :]


def _pool(z, buf16, pool_w, pool_scale, l, *, tl, start, out_dtype):
    b, t, _ = z.shape
    w = buf16.shape[2]
    assert w == len(POOL_WINDOWS) * LANE
    return pl.pallas_call(
        functools.partial(_pool_kernel, tl=tl, start=start),
        grid=(b, t // tl),
        in_specs=[
            pl.BlockSpec((None, tl, w), lambda bi, ti: (bi, ti, 0)),
            pl.BlockSpec((None, POOL_HALO, w), lambda bi, ti: (bi, 0, 0)),
            pl.BlockSpec((None, len(POOL_WINDOWS), LANE, LANE), lambda bi, ti: (l, 0, 0, 0)),
            pl.BlockSpec((None, 1, w), lambda bi, ti: (l, 0, 0)),
        ],
        out_specs=pl.BlockSpec((None, tl, w), lambda bi, ti: (bi, ti, 0)),
        out_shape=jax.ShapeDtypeStruct((b, t, w), out_dtype),
        scratch_shapes=[pltpu.VMEM((POOL_HALO + tl, w), F32)],
        compiler_params=_params(("parallel", "arbitrary")),
        name="pool",
    )(z, buf16, pool_w, pool_scale)


def _s5_kernel(u_ref, s0r_ref, s0i_ref, lr_ref, li_ref, ldt_ref, br_ref, bi_ref, cr_ref, ci_ref,
               d_ref, wg_ref, bg_ref, y_ref, sr_ref, si_ref,
               pr_ref, pi_ref, qr_ref, qi_ref, apr_ref, api_ref, zr_ref, zi_ref, car_ref, cai_ref,
               *, tl, pad, n_chunk):
    ti = pl.program_id(1)
    n_state = lr_ref.shape[1]
    n_pass = int(math.log2(tl))
    uc = u_ref.shape[1] // n_chunk
    sc = n_state // n_chunk

    @pl.when(ti == 0)
    def _():
        dt = jnp.exp(ldt_ref[...])
        lr, li = lr_ref[...], li_ref[...]
        mag = jnp.exp(lr * dt)
        a_re, a_im = mag * jnp.cos(li * dt), mag * jnp.sin(li * dt)
        den = lr * lr + li * li
        zr_ref[...] = ((a_re - 1.0) * lr + a_im * li) / den
        zi_ref[...] = (a_im * lr - (a_re - 1.0) * li) / den
        pw_re, pw_im = a_re, a_im
        for j in range(n_pass):
            apr_ref[j:j + 1, :] = pw_re
            api_ref[j:j + 1, :] = pw_im
            pw_re, pw_im = pw_re * pw_re - pw_im * pw_im, 2.0 * (pw_re * pw_im)
        car_ref[...] = s0r_ref[...]
        cai_ref[...] = s0i_ref[...]
        zeros = jnp.zeros((pad, n_state), F32)
        pr_ref[0:pad, :] = zeros
        pi_ref[0:pad, :] = zeros
        qr_ref[0:pad, :] = zeros
        qi_ref[0:pad, :] = zeros

    u = u_ref[...]
    for c in range(n_chunk):
        uu = u[:, c * uc:(c + 1) * uc]
        ub_re = _dot(uu, br_ref[c])
        ub_im = _dot(uu, bi_ref[c])
        cols = slice(c * sc, (c + 1) * sc)
        z_re, z_im = zr_ref[:, cols], zi_ref[:, cols]
        pr_ref[pad:pad + tl, cols] = z_re * ub_re - z_im * ub_im
        pi_ref[pad:pad + tl, cols] = z_re * ub_im + z_im * ub_re
    a_re, a_im = apr_ref[0:1, :], api_ref[0:1, :]
    c_re, c_im = car_ref[...], cai_ref[...]
    pr_ref[pad:pad + 1, :] += a_re * c_re - a_im * c_im
    pi_ref[pad:pad + 1, :] += a_re * c_im + a_im * c_re

    bufs = ((pr_ref, pi_ref), (qr_ref, qi_ref))
    for j in range(n_pass):
        k = 1 << j
        (s_re, s_im), (d_re, d_im) = bufs[j % 2], bufs[(j + 1) % 2]

        def col_body(cb, carry, k=k, j=j, s_re=s_re, s_im=s_im, d_re=d_re, d_im=d_im):
            for sub in range(SCAN_COLS):
                cols = pl.ds(pl.multiple_of((cb * SCAN_COLS + sub) * LANE, LANE), LANE)
                ar, ai = apr_ref[j:j + 1, cols], api_ref[j:j + 1, cols]
                cur_re, cur_im = s_re[pad:pad + tl, cols], s_im[pad:pad + tl, cols]
                sh_re, sh_im = s_re[pad - k:pad - k + tl, cols], s_im[pad - k:pad - k + tl, cols]
                d_re[pad:pad + tl, cols] = cur_re + (ar * sh_re - ai * sh_im)
                d_im[pad:pad + tl, cols] = cur_im + (ar * sh_im + ai * sh_re)
            return carry

        lax.fori_loop(0, n_state // (SCAN_COLS * LANE), col_body, 0)
    f_re, f_im = bufs[n_pass % 2]

    car_ref[...] = f_re[pad + tl - 1:pad + tl, :]
    cai_ref[...] = f_im[pad + tl - 1:pad + tl, :]
    sr_ref[...] = car_ref[...]
    si_ref[...] = cai_ref[...]

    ys = []
    for c in range(n_chunk):
        cols = slice(c * sc, (c + 1) * sc)
        ys.append(_dot(f_re[pad:pad + tl, cols], cr_ref[c]) - _dot(f_im[pad:pad + tl, cols], ci_ref[c]))
    y = jnp.concatenate(ys, axis=1) + d_ref[...] * u
    y = 0.5 * y * (1.0 + jnp.tanh(math.sqrt(2.0 / math.pi) * (y + 0.044715 * (y * y * y))))
    y_ref[...] = (y * _sigmoid(_dot(y, wg_ref[...]) + bg_ref[...])).astype(y_ref.dtype)


def _s5(z, col0, s0_re, s0_im, lam_re, lam_im, ldt_rep, b_re_c, b_im_c, c_re_c, c_im_c, ssm_d, glu_w, glu_b, l,
        *, tl, out_dtype):
    b, t, _ = z.shape
    n_state = s0_re.shape[2]
    n_chunk, uc, sc = b_re_c.shape[1:]
    u_w = n_chunk * uc
    assert col0 % u_w == 0 and n_chunk * sc == n_state
    pad = max(8, tl // 2)
    n_pass = int(math.log2(tl))
    assert 1 << n_pass == tl
    vec = lambda w: pl.BlockSpec((None, 1, w), lambda bi, ti: (l, 0, 0))
    state_spec = pl.BlockSpec((None, 1, n_state), lambda bi, ti: (bi, 0, 0))
    return pl.pallas_call(
        functools.partial(_s5_kernel, tl=tl, pad=pad, n_chunk=n_chunk),
        grid=(b, t // tl),
        in_specs=[
            pl.BlockSpec((None, tl, u_w), lambda bi, ti: (bi, ti, col0 // u_w)),
            state_spec, state_spec,
            vec(n_state), vec(n_state), vec(n_state),
            pl.BlockSpec((None, n_chunk, uc, sc), lambda bi, ti: (l, 0, 0, 0)),
            pl.BlockSpec((None, n_chunk, uc, sc), lambda bi, ti: (l, 0, 0, 0)),
            pl.BlockSpec((None, n_chunk, sc, uc), lambda bi, ti: (l, 0, 0, 0)),
            pl.BlockSpec((None, n_chunk, sc, uc), lambda bi, ti: (l, 0, 0, 0)),
            vec(u_w),
            pl.BlockSpec((None, u_w, u_w), lambda bi, ti: (l, 0, 0)),
            vec(u_w),
        ],
        out_specs=[
            pl.BlockSpec((None, tl, u_w), lambda bi, ti: (bi, ti, 0)),
            state_spec, state_spec,
        ],
        out_shape=[
            jax.ShapeDtypeStruct((b, t, u_w), out_dtype),
            jax.ShapeDtypeStruct((b, 1, n_state), F32),
            jax.ShapeDtypeStruct((b, 1, n_state), F32),
        ],
        scratch_shapes=[pltpu.VMEM((pad + tl, n_state), F32)] * 4
        + [pltpu.VMEM((max(8, n_pass), n_state), F32)] * 2
        + [pltpu.VMEM((1, n_state), F32)] * 4,
        compiler_params=_params(("parallel", "arbitrary")),
        name="s5",
    )(z, s0_re, s0_im, lam_re, lam_im, ldt_rep, b_re_c, b_im_c, c_re_c, c_im_c, ssm_d, glu_w, glu_b)


def _block_diag_chunks(w, n_chunk):
    nl, g, r, c = w.shape
    gc = g // n_chunk
    w = w.reshape(nl, n_chunk, gc, r, c)
    eye = jnp.eye(gc, dtype=w.dtype)
    out = w[:, :, :, :, None, :] * eye[None, None, :, None, :, None]
    return out.reshape(nl, n_chunk, gc * r, gc * c)


def _top_blocks(gate, n_valid_col):
    lane = lax.broadcasted_iota(jnp.int32, gate.shape, 1)
    lane_f = lane.astype(F32)
    gate = jnp.where(lane < n_valid_col, gate, NEG_INF)
    sel = jnp.zeros(gate.shape, F32)
    for _ in range(MOBA_TOPK):
        m = jnp.max(gate, axis=1, keepdims=True)
        idx = jnp.min(jnp.where(gate == m, lane_f, float(gate.shape[1])), axis=1, keepdims=True)
        hit = lane_f == idx
        valid = jnp.broadcast_to(m, gate.shape) > 0.5 * NEG_INF
        sel = jnp.where(hit & valid, 1.0, sel)
        gate = jnp.where(hit, NEG_INF, gate)
    return sel


def _pick_column(sel_b, j):
    k = lax.broadcasted_iota(jnp.int32, (LANE, LANE), 0)
    onehot = jnp.where(k == j, 1.0, 0.0).astype(BF16)
    return jnp.dot(sel_b, onehot, preferred_element_type=F32) > 0.5


def _lane_tile(x, k):
    return x if k == 1 else jnp.concatenate([x] * k, axis=1)


def _moba_prompt_kernel(slope_ref, q_ref, k_ref, v_ref, o_ref, kb_ref, vb_ref, km_ref, *, n_blk, hp, cb, scale):
    hg = pl.program_id(1)
    n = pl.program_id(2)
    blk = MOBA_BLOCK
    wc = cb * blk
    rows = hp * blk
    t = k_ref.shape[0]
    stack = lambda per_head: jnp.concatenate(per_head, axis=0)
    head_rows = lambda x, hh: x[hh * blk:(hh + 1) * blk]

    @pl.when(n == 0)
    def _():
        km_ref[...] = jnp.zeros_like(km_ref)
        for hh in range(hp):
            cols = slice(hh * HEAD_DIM, (hh + 1) * HEAD_DIM)
            kb_ref[hh] = k_ref[:, cols].astype(BF16)
            vb_ref[hh, :, 0:HEAD_DIM] = v_ref[:, cols].astype(BF16)
            vb_ref[hh, :, HEAD_DIM:] = jnp.ones((t, HEAD_DIM), BF16)
            for j in range(n_blk):
                km_ref[hh, j:j + 1, :] = jnp.mean(k_ref[j * blk:(j + 1) * blk, cols], axis=0, keepdims=True)

    slopes = [slope_ref[hg * hp + hh] for hh in range(hp)]
    cpos = lax.broadcasted_iota(jnp.int32, (1, wc), 1).astype(F32)
    col_bias = [s * cpos for s in slopes]
    slope_rows = stack([jnp.zeros((blk, LANE), F32) + s for s in slopes])
    qs = [q_ref[:, hh * HEAD_DIM:(hh + 1) * HEAD_DIM] for hh in range(hp)]
    qbs = [(q * scale).astype(BF16) for q in qs]
    gate = stack([_dot_nt(q, km_ref[hh]) for hh, q in enumerate(qs)])
    sel_b = _top_blocks(gate, n).astype(BF16)
    rowmax = lambda x: jnp.broadcast_to(jnp.max(x, axis=1, keepdims=True), (rows, LANE))

    own0 = pl.multiple_of(n * blk, blk)
    causal = stack([lax.broadcasted_iota(jnp.int32, (blk, blk), 0)
                    >= lax.broadcasted_iota(jnp.int32, (blk, blk), 1)] * hp)
    s = stack([_dot_nt(qb, kb_ref[hh, pl.ds(own0, blk), :]) + col_bias[hh][:, :blk] for hh, qb in enumerate(qbs)])
    s = jnp.where(causal, s, NEG_INF)
    m0 = rowmax(s)
    p = jnp.exp(s - _lane_tile(m0, blk // LANE))
    acc0 = stack([_dot(head_rows(p, hh), vb_ref[hh, pl.ds(own0, blk), :]) for hh in range(hp)])

    def body(c, carry):
        m, acc = carry
        c0 = pl.multiple_of(c * wc, wc)
        shift = slope_rows * (n * blk - c * wc).astype(F32)
        s = stack([_dot_nt(qb, kb_ref[hh, pl.ds(c0, wc), :]) + col_bias[hh] for hh, qb in enumerate(qbs)])
        groups = [s[:, g * blk:(g + 1) * blk] for g in range(cb)]
        chosen = [_pick_column(sel_b, c * cb + g) for g in range(cb)]
        m_new = m
        for g in range(cb):
            m_new = jnp.maximum(m_new, jnp.where(chosen[g], rowmax(groups[g]) - shift, NEG_INF))
        p = jnp.concatenate(
            [jnp.exp(groups[g] - _lane_tile(jnp.where(chosen[g], m_new + shift, -NEG_INF), blk // LANE))
             for g in range(cb)], axis=1)
        pv = stack([_dot(head_rows(p, hh), vb_ref[hh, pl.ds(c0, wc), :]) for hh in range(hp)])
        return m_new, _lane_tile(jnp.exp(m - m_new), 2 * HEAD_DIM // LANE) * acc + pv

    _, acc = lax.fori_loop(0, (n + cb - 1) // cb, body, (m0, acc0))
    for hh in range(hp):
        a = head_rows(acc, hh)
        o_ref[:, hh * HEAD_DIM:(hh + 1) * HEAD_DIM] = (a[:, :HEAD_DIM] / a[:, HEAD_DIM:]).astype(o_ref.dtype)


def _moba_prompt(z, slopes, q_col, k_col, v_col, n_heads, *, hp, cb):
    b, t, _ = z.shape
    n_blk = t // MOBA_BLOCK
    w = hp * HEAD_DIM
    assert n_blk * MOBA_BLOCK == t and n_blk <= LANE and n_heads % hp == 0 and n_blk % cb == 0
    assert q_col % w == 0 and k_col % w == 0 and v_col % w == 0
    qc, kc, vc = q_col // w, k_col // w, v_col // w
    whole = lambda c: pl.BlockSpec((None, t, w), lambda bi, hg, n, s: (bi, 0, c + hg),
                                   pipeline_mode=pl.Buffered(1))
    return pl.pallas_call(
        functools.partial(_moba_prompt_kernel, n_blk=n_blk, hp=hp, cb=cb, scale=HEAD_DIM ** -0.5),
        grid_spec=pltpu.PrefetchScalarGridSpec(
            num_scalar_prefetch=1,
            grid=(b, n_heads // hp, n_blk),
            in_specs=[
                pl.BlockSpec((None, MOBA_BLOCK, w), lambda bi, hg, n, s: (bi, n, qc + hg)),
                whole(kc), whole(vc),
            ],
            out_specs=pl.BlockSpec((None, MOBA_BLOCK, w), lambda bi, hg, n, s: (bi, n, hg)),
            scratch_shapes=[pltpu.VMEM((hp, t, HEAD_DIM), BF16), pltpu.VMEM((hp, t, 2 * HEAD_DIM), BF16),
                            pltpu.VMEM((hp, LANE, HEAD_DIM), F32)],
        ),
        out_shape=jax.ShapeDtypeStruct((b, t, n_heads * HEAD_DIM), BF16),
        compiler_params=_params(("parallel", "parallel", "arbitrary")),
        name="moba_prompt",
    )(slopes, z, z, z)


def _page_spec(n_heads, index_map):
    return pl.BlockSpec((None, None, PAGE_SIZE * n_heads, HEAD_DIM), index_map)


def _kmean_kernel(pt_ref, *refs, n_heads, bps):
    pages, o_ref = refs[:PAGES_PER_BLOCK * bps], refs[PAGES_PER_BLOCK * bps]
    jj = pl.program_id(2)
    for bi in range(bps):
        tot = jnp.zeros((n_heads, HEAD_DIM), F32)
        for pg in range(PAGES_PER_BLOCK):
            page = pages[PAGES_PER_BLOCK * bi + pg][...].reshape(PAGE_SIZE, n_heads, HEAD_DIM)
            tot = tot + jnp.sum(page, axis=0)
        row0 = pl.multiple_of((jj * bps + bi) * n_heads, n_heads)
        o_ref[pl.ds(row0, n_heads), :] = tot * (1.0 / MOBA_BLOCK)


def _paged_block_means(cache, page_table, n_heads, *, bps):
    nl = cache.shape[0]
    b, n_pages = page_table.shape
    n_blk = n_pages // PAGES_PER_BLOCK
    assert n_blk % bps == 0

    def spec(k):
        return _page_spec(n_heads, lambda l, bi, jj, pt: (l, pt[bi, PAGES_PER_BLOCK * bps * jj + k], 0, 0))

    return pl.pallas_call(
        functools.partial(_kmean_kernel, n_heads=n_heads, bps=bps),
        grid_spec=pltpu.PrefetchScalarGridSpec(
            num_scalar_prefetch=1,
            grid=(nl, b, n_blk // bps),
            in_specs=[spec(k) for k in range(PAGES_PER_BLOCK * bps)],
            out_specs=pl.BlockSpec((None, None, n_blk * n_heads, HEAD_DIM), lambda l, bi, jj, pt: (l, bi, 0, 0)),
        ),
        out_shape=jax.ShapeDtypeStruct((nl, b, n_blk * n_heads, HEAD_DIM), F32),
        compiler_params=_params(("parallel", "parallel", "arbitrary")),
        name="paged_block_means",
    )(page_table, *([cache] * (PAGES_PER_BLOCK * bps)))


def _head_major(page_ref, n_heads):
    return jnp.concatenate([page_ref[pl.ds(h, PAGE_SIZE, stride=n_heads), :] for h in range(n_heads)], axis=1)


def _moba_sample_kernel(pt_ref, slope_ref, q_ref, kn_ref, vn_ref, km_ref, *refs,
                        n_heads, n_blk, bps, past_len, scale):
    npg = PAGES_PER_BLOCK * bps
    k_refs, v_refs = refs[:npg], refs[npg:2 * npg]
    o_ref, qbd_ref, sel_ref, slp_ref, m_ref, l_ref, acc_ref = refs[2 * npg:]
    jj = pl.program_id(1)
    t, width = q_ref.shape
    rows = n_heads * t
    rid = lax.broadcasted_iota(jnp.int32, (rows, 1), 0).astype(F32)
    hq = jnp.floor((rid + 0.5) * (1.0 / t))
    tq = rid - t * hq

    @pl.when(jj == 0)
    def _():
        q = q_ref[...]
        c = lax.broadcasted_iota(jnp.int32, (t, width), 1)
        per_head = [jnp.where((c >= h * HEAD_DIM) & (c < (h + 1) * HEAD_DIM), q, 0.0) for h in range(n_heads)]
        qbd_ref[...] = (jnp.concatenate(per_head, axis=0) * scale).astype(BF16)
        gates = []
        for h in range(n_heads):
            km_h = km_ref[pl.ds(h, n_blk, stride=n_heads), :]
            km_h = jnp.concatenate([km_h, jnp.zeros((LANE - n_blk, HEAD_DIM), F32)], axis=0)
            gates.append(_dot_nt(q[:, h * HEAD_DIM:(h + 1) * HEAD_DIM], km_h))
        sel_ref[...] = _top_blocks(jnp.concatenate(gates, axis=0), n_blk).astype(BF16)
        slp = jnp.zeros((rows, 1), F32)
        for h in range(n_heads):
            slp = jnp.where(hq == h, slope_ref[h], slp)
        slp_ref[...] = slp
        fill = jnp.zeros((LANE - t, width), F32)
        s = _dot_nt(qbd_ref[...], jnp.concatenate([kn_ref[...], fill], axis=0))
        cpos = lax.broadcasted_iota(jnp.int32, (rows, LANE), 1).astype(F32)
        s = jnp.where(cpos <= tq, s - slp * (tq - cpos), NEG_INF)
        m0 = jnp.max(s, axis=1, keepdims=True)
        p = jnp.exp(s - m0)
        m_ref[...] = m0
        l_ref[...] = jnp.sum(p, axis=1, keepdims=True)
        acc_ref[...] = _dot(p, jnp.concatenate([vn_ref[...], fill], axis=0))

    slp = slp_ref[...]
    qbd = qbd_ref[...]
    sel_b = sel_ref[...]
    kmat = jnp.concatenate([_head_major(r, n_heads) for r in k_refs], axis=0).astype(BF16)
    vmat = jnp.concatenate([_head_major(r, n_heads) for r in v_refs], axis=0).astype(BF16)
    kpos = lax.broadcasted_iota(jnp.int32, (rows, bps * MOBA_BLOCK), 1).astype(F32)
    s = _dot_nt(qbd, kmat) - slp * (tq - kpos)
    shift = slp * (past_len - jj * (bps * MOBA_BLOCK)).astype(F32)
    groups = [s[:, g * MOBA_BLOCK:(g + 1) * MOBA_BLOCK] for g in range(bps)]
    chosen = [_pick_column(sel_b, jj * bps + g)[:, 0:1] for g in range(bps)]
    m = m_ref[...]
    m_new = m
    for g in range(bps):
        m_new = jnp.maximum(m_new, jnp.where(chosen[g], jnp.max(groups[g], axis=1, keepdims=True) - shift, NEG_INF))
    p = jnp.concatenate([jnp.exp(groups[g] - jnp.where(chosen[g], m_new + shift, -NEG_INF)) for g in range(bps)],
                        axis=1)
    corr = jnp.exp(m - m_new)
    lsum = corr * l_ref[...] + jnp.sum(p, axis=1, keepdims=True)
    acc = corr * acc_ref[...] + _dot(p, vmat)
    m_ref[...], l_ref[...], acc_ref[...] = m_new, lsum, acc

    @pl.when(jj == pl.num_programs(1) - 1)
    def _():
        out = acc / lsum
        o_ref[...] = jnp.concatenate(
            [out[h * t:(h + 1) * t, h * HEAD_DIM:(h + 1) * HEAD_DIM] for h in range(n_heads)], axis=1)


def _moba_sample(z, kmeans, slopes, cache_k, cache_v, page_table, l, q_col, k_col, v_col, n_heads, *, bps):
    b, t, _ = z.shape
    n_pages = page_table.shape[1]
    n_blk = n_pages // PAGES_PER_BLOCK
    width = n_heads * HEAD_DIM
    npg = PAGES_PER_BLOCK * bps
    rows = n_heads * t
    assert n_blk % bps == 0 and n_blk <= LANE and t <= LANE and rows % 16 == 0
    assert q_col % width == 0 and k_col % width == 0 and v_col % width == 0

    def page(k):
        return _page_spec(n_heads, lambda bi, jj, pt, sl: (l, pt[bi, npg * jj + k], 0, 0))

    tok = lambda col: pl.BlockSpec((None, t, width), lambda bi, jj, pt, sl: (bi, 0, col // width))
    return pl.pallas_call(
        functools.partial(_moba_sample_kernel, n_heads=n_heads, n_blk=n_blk, bps=bps,
                          past_len=n_pages * PAGE_SIZE, scale=HEAD_DIM ** -0.5),
        grid_spec=pltpu.PrefetchScalarGridSpec(
            num_scalar_prefetch=2,
            grid=(b, n_blk // bps),
            in_specs=[tok(q_col), tok(k_col), tok(v_col),
                      pl.BlockSpec((None, None, n_blk * n_heads, HEAD_DIM), lambda bi, jj, pt, sl: (l, bi, 0, 0))]
            + [page(k) for k in range(npg)] + [page(k) for k in range(npg)],
            out_specs=pl.BlockSpec((None, t, width), lambda bi, jj, pt, sl: (bi, 0, 0)),
            scratch_shapes=[pltpu.VMEM((rows, width), BF16), pltpu.VMEM((rows, LANE), BF16),
                            pltpu.VMEM((rows, 1), F32), pltpu.VMEM((rows, 1), F32), pltpu.VMEM((rows, 1), F32),
                            pltpu.VMEM((rows, width), F32)],
        ),
        out_shape=jax.ShapeDtypeStruct((b, t, width), F32),
        compiler_params=_params(("parallel", "arbitrary")),
        name="moba_sample",
    )(page_table, slopes, z, z, z, kmeans, *([cache_k] * npg), *([cache_v] * npg))


def _merge_ln_kernel(a_ref, s_ref, c_ref, g0_ref, g1_ref, g2_ref, bg0_ref, bg1_ref, bg2_ref,
                     wp_ref, ws_ref, wa_ref, wo_ref, h_ref, g_ref, b_ref, o_ref, *, alpha):
    j = pl.program_id(1)

    @pl.when(j == 0)
    def _():
        o_ref[...] = jnp.zeros_like(o_ref)

    gated = lambda g_ref, bg_ref, x_ref, w_ref: (
        (jnp.tanh(0.5 * (g_ref[...] + bg_ref[...])) + 1.0) * _dot(x_ref[...], w_ref[...]))
    merged = 0.5 * (gated(g0_ref, bg0_ref, a_ref, wp_ref) + gated(g1_ref, bg1_ref, s_ref, ws_ref)
                    + gated(g2_ref, bg2_ref, c_ref, wa_ref))
    o_ref[...] += _dot(merged, wo_ref[...])

    @pl.when(j == pl.num_programs(1) - 1)
    def _():
        o_ref[...] = _layer_norm(alpha * h_ref[...] + o_ref[...], g_ref[...], b_ref[...])


def _merge_ln(a_out, s_out, c_out, z, gate_col, b_gate, wp, ws, wa, wo, h, ln_g, ln_b, l, ln_idx,
              *, alpha, tm, tn):
    m, d = h.shape
    gc = gate_col // tn
    nj = d // tn
    row = lambda w: pl.BlockSpec((tm, w), lambda i, j: (i, 0), pipeline_mode=pl.Buffered(1))
    gate = lambda g: pl.BlockSpec((tm, tn), lambda i, j: (i, gc + g * nj + j))
    bias = lambda g: pl.BlockSpec((None, 1, tn), lambda i, j: (l, 0, g * nj + j))
    wcol = lambda k: pl.BlockSpec((None, k, tn), lambda i, j: (l, 0, j))
    return pl.pallas_call(
        functools.partial(_merge_ln_kernel, alpha=alpha),
        grid=(m // tm, nj),
        in_specs=[
            row(a_out.shape[1]), row(s_out.shape[1]), row(c_out.shape[1]),
            gate(0), gate(1), gate(2), bias(0), bias(1), bias(2),
            wcol(a_out.shape[1]), wcol(s_out.shape[1]), wcol(c_out.shape[1]),
            pl.BlockSpec((None, tn, d), lambda i, j: (l, j, 0)),
            row(d),
            pl.BlockSpec((None, 1, d), lambda i, j: (ln_idx, 0, 0)),
            pl.BlockSpec((None, 1, d), lambda i, j: (ln_idx, 0, 0)),
        ],
        out_specs=pl.BlockSpec((tm, d), lambda i, j: (i, 0), pipeline_mode=pl.Buffered(1)),
        out_shape=jax.ShapeDtypeStruct((m, d), F32),
        compiler_params=_params(("parallel", "arbitrary")),
        name="merge_ln",
    )(a_out, s_out, c_out, z, z, z, b_gate, b_gate, b_gate, wp, ws, wa, wo, h, ln_g, ln_b)


def _memattn_ln_kernel(h_ref, wq_ref, mk_ref, mv_ref, wo_ref, g_ref, b_ref, o_ref, *, alpha, scale):
    hx = h_ref[...]
    q = _dot(hx, wq_ref[...])
    mk, mv = mk_ref[...], mv_ref[...]
    outs = []
    for hd in range(N_MEM_HEADS):
        cols = slice(hd * MEM_HEAD_DIM, (hd + 1) * MEM_HEAD_DIM)
        s = _dot_nt(q[:, cols], mk[:, cols]) * scale
        s = s - jnp.max(s, axis=1, keepdims=True)
        p = jnp.exp(s)
        p = p / jnp.sum(p, axis=1, keepdims=True)
        outs.append(_dot(p, mv[:, cols]))
    o = jnp.concatenate(outs, axis=1)
    o_ref[...] = _layer_norm(alpha * hx + _dot(o, wo_ref[...]), g_ref[...], b_ref[...])


def _memattn_ln(h, mem_k, mem_v, mem_l, wq, wo, ln_g, ln_b, l, ln_idx, *, alpha, tm):
    b, t, d = h.shape
    mlen, w = mem_k.shape[2:]
    mem_spec = pl.BlockSpec((None, None, mlen, w), lambda bi, ti: (mem_l, bi, 0, 0))
    return pl.pallas_call(
        functools.partial(_memattn_ln_kernel, alpha=alpha, scale=MEM_HEAD_DIM ** -0.5),
        grid=(b, t // tm),
        in_specs=[
            pl.BlockSpec((None, tm, d), lambda bi, ti: (bi, ti, 0)),
            pl.BlockSpec((None, d, w), lambda bi, ti: (l, 0, 0)),
            mem_spec, mem_spec,
            pl.BlockSpec((None, w, d), lambda bi, ti: (l, 0, 0)),
            pl.BlockSpec((None, 1, d), lambda bi, ti: (ln_idx, 0, 0)),
            pl.BlockSpec((None, 1, d), lambda bi, ti: (ln_idx, 0, 0)),
        ],
        out_specs=pl.BlockSpec((None, tm, d), lambda bi, ti: (bi, ti, 0)),
        out_shape=jax.ShapeDtypeStruct((b, t, d), F32),
        compiler_params=_params(("parallel", "parallel")),
        name="memattn_ln",
    )(h, wq, mem_k, mem_v, wo, ln_g, ln_b)


def kernel(x_prompt, x_sample, cache_k, cache_v, cache_mem_k, cache_mem_v, state_pool, state_ssm_re, state_ssm_im, page_table, mem_prompt, ln_g, ln_b, ffn1_gate, ffn1_up, ffn1_down, w_in, b_gate, pool_w, pool_scale, ssm_lam_re, ssm_lam_im, ssm_log_dt, ssm_b_re, ssm_b_im, ssm_c_re, ssm_c_im, ssm_d, ssm_glu_w, ssm_glu_b, w_proj_pool, w_proj_ssm, w_proj_attn, w_out, mem_wq, mem_wk, mem_wv, mem_wo, ffn2_gate, ffn2_up, ffn2_down):
    depth = ln_g.shape[0]
    bp, tp, d = x_prompt.shape
    bs, ts, _ = x_sample.shape
    pool_width = state_pool.shape[3]
    n_groups, n_state = state_ssm_re.shape[2:]
    ssm_width = n_groups * SSM_GROUP
    n_heads = cache_k.shape[3]
    attn_width = n_heads * HEAD_DIM
    n_pages = page_table.shape[1]
    past_len = n_pages * PAGE_SIZE
    n_past_blk = n_pages // PAGES_PER_BLOCK
    mem_len = mem_prompt.shape[1]
    mem_width = N_MEM_HEADS * MEM_HEAD_DIM
    alpha = (2 * depth) ** 0.25
    ssm_col = pool_width
    q_col = ssm_col + ssm_width
    k_col = q_col + attn_width
    v_col = k_col + attn_width
    gate_col = v_col + attn_width

    ln_g2 = ln_g.reshape(depth * 4, 1, d)
    ln_b2 = ln_b.reshape(depth * 4, 1, d)
    b_gate2 = b_gate.reshape(depth, 1, N_BRANCH * d)
    pool_scale2 = pool_scale.reshape(depth, 1, pool_width)
    n_chunk = 4
    lam_re2 = ssm_lam_re.reshape(depth, 1, n_groups * n_state)
    lam_im2 = ssm_lam_im.reshape(depth, 1, n_groups * n_state)
    ldt_rep = jnp.repeat(ssm_log_dt, n_state, axis=1).reshape(depth, 1, n_groups * n_state)
    b_re_c = _block_diag_chunks(jnp.swapaxes(ssm_b_re, 2, 3), n_chunk)
    b_im_c = _block_diag_chunks(jnp.swapaxes(ssm_b_im, 2, 3), n_chunk)
    c_re_c = _block_diag_chunks(jnp.swapaxes(ssm_c_re, 2, 3), n_chunk)
    c_im_c = _block_diag_chunks(jnp.swapaxes(ssm_c_im, 2, 3), n_chunk)
    ssm_d2 = ssm_d.reshape(depth, 1, ssm_width)
    glu_b2 = ssm_glu_b.reshape(depth, 1, ssm_width)
    slopes = jnp.exp2(-8.0 * jnp.arange(1, n_heads + 1, dtype=F32) / n_heads)
    cache_k4 = cache_k.reshape(cache_k.shape[0], cache_k.shape[1], PAGE_SIZE * n_heads, HEAD_DIM)
    cache_v4 = cache_v.reshape(cache_v.shape[0], cache_v.shape[1], PAGE_SIZE * n_heads, HEAD_DIM)
    cmem_k = cache_mem_k.reshape(depth, bs, mem_len, mem_width)
    cmem_v = cache_mem_v.reshape(depth, bs, mem_len, mem_width)
    mem_rows = mem_prompt.reshape(bp * mem_len, d)
    zero_pool = jnp.zeros((bp, POOL_HALO, pool_width), F32)
    zero_state = jnp.zeros((bp, 1, n_groups * n_state), F32)
    state_pool16 = jnp.pad(state_pool, ((0, 0), (0, 0), (POOL_HALO - POOL_BUF, 0), (0, 0)))
    s_re_in = state_ssm_re.reshape(depth, bs, 1, n_groups * n_state)
    s_im_in = state_ssm_im.reshape(depth, bs, 1, n_groups * n_state)

    merge_w = [w.astype(BF16) for w in (w_proj_pool, w_proj_ssm, w_proj_attn, w_out)]

    kmeans = _paged_block_means(cache_k4, page_table, n_heads, bps=4)

    hp = x_prompt.reshape(bp * tp, d)
    hs = x_sample.reshape(bs * ts, d)
    mp, ms = bp * tp, bs * ts
    outs = {k: [] for k in ("poolp", "rep", "imp", "mkp", "mvp", "ks", "vs", "pools", "res", "ims")}
    kv_stacks = None

    def mix_common(h, z3, a_out, s_out, c_out, l, tm):
        m = h.shape[0]
        return _merge_ln(a_out.reshape(m, -1), s_out.reshape(m, -1), c_out.reshape(m, -1),
                         z3.reshape(m, -1), gate_col, b_gate2, *merge_w,
                         h, ln_g2, ln_b2, l, 4 * l + 1, alpha=alpha, tm=tm, tn=512)

    for l in range(depth):
        mk_p = _matmul(mem_rows, mem_wk, l, tm=bp * mem_len, tn=mem_width)
        mv_p = _matmul(mem_rows, mem_wv, l, tm=bp * mem_len, tn=mem_width)
        hp, hs = _ffn_ln(hp, hs, ffn1_gate, ffn1_up, ffn1_down, ln_g2, ln_b2, l, 4 * l, alpha=alpha,
                         tm=1024, tf=256)
        zp, *kv_stacks = _in_proj(hp, w_in, l, k_col, v_col, attn_width, kv_stacks, tm=2048, tn=512)
        zp = zp.reshape(bp, tp, -1)
        a_out = _pool(zp, zero_pool, pool_w, pool_scale2, l, tl=256, start=0, out_dtype=BF16)
        s_out, re_p, im_p = _s5(zp, ssm_col, zero_state, zero_state, lam_re2, lam_im2, ldt_rep,
                                b_re_c, b_im_c, c_re_c, c_im_c, ssm_d2, ssm_glu_w, glu_b2, l, tl=256,
                                out_dtype=BF16)
        c_out = _moba_prompt(zp, slopes, q_col, k_col, v_col, n_heads, hp=4, cb=2)
        hp = mix_common(hp, zp, a_out, s_out, c_out, l, 1024)
        hp = _memattn_ln(hp.reshape(bp, tp, d), mk_p.reshape(1, bp, mem_len, mem_width),
                         mv_p.reshape(1, bp, mem_len, mem_width), 0, mem_wq, mem_wo, ln_g2, ln_b2, l, 4 * l + 2,
                         alpha=alpha, tm=512).reshape(mp, d)
        outs["poolp"].append(zp[:, tp - POOL_BUF:, :pool_width])
        outs["rep"].append(re_p.reshape(bp, n_groups, n_state))
        outs["imp"].append(im_p.reshape(bp, n_groups, n_state))
        outs["mkp"].append(mk_p.reshape(bp, mem_len, N_MEM_HEADS, MEM_HEAD_DIM))
        outs["mvp"].append(mv_p.reshape(bp, mem_len, N_MEM_HEADS, MEM_HEAD_DIM))

        zs = _matmul(hs, w_in, l, tm=ms, tn=512).reshape(bs, ts, -1)
        a_out = _pool(zs, state_pool16[l], pool_w, pool_scale2, l, tl=ts, start=past_len, out_dtype=F32)
        s_out, re_s, im_s = _s5(zs, ssm_col, s_re_in[l], s_im_in[l], lam_re2, lam_im2, ldt_rep,
                                b_re_c, b_im_c, c_re_c, c_im_c, ssm_d2, ssm_glu_w, glu_b2, l, tl=ts,
                                out_dtype=F32)
        c_out = _moba_sample(zs, kmeans, slopes, cache_k4, cache_v4, page_table, l, q_col, k_col, v_col,
                             n_heads, bps=4)
        hs = mix_common(hs, zs, a_out, s_out, c_out, l, ms)
        hs = _memattn_ln(hs.reshape(bs, ts, d), cmem_k, cmem_v, l, mem_wq, mem_wo, ln_g2, ln_b2, l, 4 * l + 2,
                         alpha=alpha, tm=ts).reshape(ms, d)
        hp, hs = _ffn_ln(hp, hs, ffn2_gate, ffn2_up, ffn2_down, ln_g2, ln_b2, l, 4 * l + 3, alpha=alpha,
                         tm=1024, tf=256)
        outs["ks"].append(zs[:, :, k_col:v_col].reshape(bs, ts, n_heads, HEAD_DIM))
        outs["vs"].append(zs[:, :, v_col:gate_col].reshape(bs, ts, n_heads, HEAD_DIM))
        outs["pools"].append(jnp.concatenate([state_pool[l], zs[:, :, :pool_width]], axis=1)[:, -POOL_BUF:])
        outs["res"].append(re_s.reshape(bs, n_groups, n_state))
        outs["ims"].append(im_s.reshape(bs, n_groups, n_state))

    st = lambda k: jnp.stack(outs[k])
    kp, vp = (s.reshape(depth, bp, tp, n_heads, HEAD_DIM) for s in kv_stacks)
    return (hp.reshape(bp, tp, d), hs.reshape(bs, ts, d),
            kp, vp, st("poolp"), st("rep"), st("imp"), st("mkp"), st("mvp"),
            st("ks"), st("vs"), st("pools"), st("res"), st("ims"))
```

```python
import functools
import math

import jax
import jax.numpy as jnp
from jax import lax
from jax.experimental import pallas as pl
from jax.experimental.pallas import tpu as pltpu

F32 = jnp.float32
BF16 = jnp.bfloat16

POOL_WINDOWS = (2, 4, 8, 16)
POOL_BUF = max(POOL_WINDOWS) - 1
POOL_HALO = 16
SSM_GROUP = 16
SSM_STATE = 64
HEAD_DIM = 128
MOBA_BLOCK = 256
MOBA_TOPK = 3
PAGE_SIZE = 128
PAGES_PER_BLOCK = MOBA_BLOCK // PAGE_SIZE
N_MEM_HEADS = 4
MEM_HEAD_DIM = 128
N_BRANCH = 3
LN_EPS = 1e-5
NEG_INF = -1e30
LANE = 128
SUBLANES = 8
SCAN_COLS = 4
VMEM_LIMIT = 56 * 1024 * 1024


def _params(sem, vmem=VMEM_LIMIT):
    return pltpu.CompilerParams(dimension_semantics=sem, vmem_limit_bytes=vmem)


def _dot(a, b):
    return jnp.dot(a.astype(BF16), b.astype(BF16), preferred_element_type=F32)


def _dot_nt(a, b):
    return lax.dot_general(a.astype(BF16), b.astype(BF16), (((1,), (1,)), ((), ())),
                           preferred_element_type=F32)


def _split(a):
    hi = a.astype(BF16)
    lo = (a - hi.astype(F32)).astype(BF16)
    return hi, lo


def _dot3(a, b):
    ah, al = _split(a)
    bh, bl = _split(b)
    d = functools.partial(jnp.dot, preferred_element_type=F32)
    return d(ah, bh) + (d(ah, bl) + d(al, bh))


def _dot3_nt(a, b):
    ah, al = _split(a)
    bh, bl = _split(b)
    d = functools.partial(lax.dot_general, dimension_numbers=(((1,), (1,)), ((), ())),
                          preferred_element_type=F32)
    return d(ah, bh) + (d(ah, bl) + d(al, bh))


def _layer_norm(y, g, b):
    mu = jnp.mean(y, axis=-1, keepdims=True)
    d = y - mu
    var = jnp.mean(d * d, axis=-1, keepdims=True)
    return d * lax.rsqrt(var + LN_EPS) * g + b


def _sigmoid(x):
    return 0.5 * jnp.tanh(0.5 * x) + 0.5


def _ffn_ln_kernel(x_ref, xs_ref, wg_ref, wu_ref, wd_ref, g_ref, b_ref, o_ref, os_ref, xb_ref, xsb_ref, *, alpha):
    i, f = pl.program_id(0), pl.program_id(1)

    def group_step(x_ref, xb_ref, o_ref):
        @pl.when(f == 0)
        def _():
            xb_ref[...] = x_ref[...].astype(BF16)
            o_ref[...] = jnp.zeros_like(o_ref)

        xb = xb_ref[...]
        gate = jnp.dot(xb, wg_ref[...].astype(BF16), preferred_element_type=F32)
        up = jnp.dot(xb, wu_ref[...].astype(BF16), preferred_element_type=F32)
        hid = gate * _sigmoid(gate) * up
        o_ref[...] += _dot(hid, wd_ref[...])

        @pl.when(f == pl.num_programs(1) - 1)
        def _():
            y = alpha * x_ref[...] + 0.5 * o_ref[...]
            o_ref[...] = _layer_norm(y, g_ref[...], b_ref[...])

    group_step(x_ref, xb_ref, o_ref)

    @pl.when(i == 0)
    def _():
        group_step(xs_ref, xsb_ref, os_ref)


def _ffn_ln(x, xs, wg, wu, wd, ln_g, ln_b, l, ln_idx, *, alpha, tm, tf):
    m, d = x.shape
    ms = xs.shape[0]
    f = wg.shape[2]
    return pl.pallas_call(
        functools.partial(_ffn_ln_kernel, alpha=alpha),
        grid=(m // tm, f // tf),
        in_specs=[
            pl.BlockSpec((tm, d), lambda i, j: (i, 0), pipeline_mode=pl.Buffered(1)),
            pl.BlockSpec((ms, d), lambda i, j: (0, 0), pipeline_mode=pl.Buffered(1)),
            pl.BlockSpec((None, d, tf), lambda i, j: (l, 0, j)),
            pl.BlockSpec((None, d, tf), lambda i, j: (l, 0, j)),
            pl.BlockSpec((None, tf, d), lambda i, j: (l, j, 0)),
            pl.BlockSpec((None, 1, d), lambda i, j: (ln_idx, 0, 0)),
            pl.BlockSpec((None, 1, d), lambda i, j: (ln_idx, 0, 0)),
        ],
        out_specs=[
            pl.BlockSpec((tm, d), lambda i, j: (i, 0), pipeline_mode=pl.Buffered(1)),
            pl.BlockSpec((ms, d), lambda i, j: (0, 0), pipeline_mode=pl.Buffered(1)),
        ],
        out_shape=[jax.ShapeDtypeStruct((m, d), F32), jax.ShapeDtypeStruct((ms, d), F32)],
        scratch_shapes=[pltpu.VMEM((tm, d), BF16), pltpu.VMEM((ms, d), BF16)],
        compiler_params=_params(("arbitrary", "arbitrary")),
        name="ffn_ln",
    )(x, xs, wg, wu, wd, ln_g, ln_b)


def _matmul_kernel(x_ref, w_ref, o_ref, xb_ref):
    @pl.when(pl.program_id(1) == 0)
    def _():
        xb_ref[...] = x_ref[...].astype(BF16)

    o_ref[...] = jnp.dot(xb_ref[...], w_ref[...].astype(BF16), preferred_element_type=F32)


def _matmul(x, w, l, *, tm, tn):
    m, k = x.shape
    n = w.shape[2]
    return pl.pallas_call(
        _matmul_kernel,
        grid=(m // tm, n // tn),
        in_specs=[
            pl.BlockSpec((tm, k), lambda i, j: (i, 0), pipeline_mode=pl.Buffered(1)),
            pl.BlockSpec((None, k, tn), lambda i, j: (l, 0, j)),
        ],
        out_specs=pl.BlockSpec((tm, tn), lambda i, j: (i, j)),
        out_shape=jax.ShapeDtypeStruct((m, n), F32),
        scratch_shapes=[pltpu.VMEM((tm, k), BF16)],
        compiler_params=_params(("parallel", "arbitrary")),
        name="matmul",
    )(x, w)


def _in_proj_kernel(*refs, k_lo, v_lo, n_kv, aliased):
    x_ref, w_ref = refs[:2]
    z_ref, ko_ref, vo_ref, xb_ref = refs[4:] if aliased else refs[2:]
    j = pl.program_id(1)

    @pl.when(j == 0)
    def _():
        xb_ref[...] = x_ref[...].astype(BF16)

    res = jnp.dot(xb_ref[...], w_ref[...].astype(BF16), preferred_element_type=F32)
    z_ref[...] = res

    @pl.when((j >= k_lo) & (j < k_lo + n_kv))
    def _():
        ko_ref[...] = res

    @pl.when((j >= v_lo) & (j < v_lo + n_kv))
    def _():
        vo_ref[...] = res


def _in_proj(x, w, l, k_col, v_col, kv_width, stacks, *, tm, tn):
    m, k = x.shape
    nl, _, n = w.shape
    assert k_col % tn == 0 and v_col % tn == 0 and kv_width % tn == 0
    k_lo, v_lo, n_kv = k_col // tn, v_col // tn, kv_width // tn
    aliased = stacks is not None

    def kv_spec(lo):
        return pl.BlockSpec((None, tm, tn), lambda i, j: (l, i, jnp.clip(j - lo, 0, n_kv - 1)),
                            pipeline_mode=pl.Buffered(1))

    stack_shape = jax.ShapeDtypeStruct((nl, m, kv_width), F32)
    return pl.pallas_call(
        functools.partial(_in_proj_kernel, k_lo=k_lo, v_lo=v_lo, n_kv=n_kv, aliased=aliased),
        grid=(m // tm, n // tn),
        in_specs=[
            pl.BlockSpec((tm, k), lambda i, j: (i, 0), pipeline_mode=pl.Buffered(1)),
            pl.BlockSpec((None, k, tn), lambda i, j: (l, 0, j)),
        ] + ([pl.BlockSpec(memory_space=pl.ANY)] * 2 if aliased else []),
        out_specs=[pl.BlockSpec((tm, tn), lambda i, j: (i, j)), kv_spec(k_lo), kv_spec(v_lo)],
        out_shape=[jax.ShapeDtypeStruct((m, n), F32), stack_shape, stack_shape],
        input_output_aliases={2: 1, 3: 2} if aliased else {},
        scratch_shapes=[pltpu.VMEM((tm, k), BF16)],
        compiler_params=_params(("arbitrary", "arbitrary")),
        name="in_proj",
    )(x, w, *(stacks if aliased else ()))


def _pool_kernel(u_ref, buf_ref, w_ref, scale_ref, o_ref, xx_ref, *, tl, start):
    ti = pl.program_id(1)

    @pl.when(ti == 0)
    def _():
        xx_ref[0:POOL_HALO, :] = buf_ref[...]

    u = u_ref[...]
    xx_ref[POOL_HALO:POOL_HALO + tl, :] = u
    pos = start + ti * tl + lax.broadcasted_iota(jnp.int32, (tl, 1), 0)
    for gi, w in enumerate(POOL_WINDOWS):
        cols = slice(gi * LANE, (gi + 1) * LANE)
        win = xx_ref[POOL_HALO:POOL_HALO + tl, cols]
        for i in range(1, w):
            win = win + xx_ref[POOL_HALO - i:POOL_HALO - i + tl, cols]
        cnt = jnp.minimum(pos + 1, w).astype(F32)
        pooled = win / cnt - u[:, cols]
        o_ref[:, cols] = (_dot3(pooled, w_ref[gi]) * scale_ref[:, cols]).astype(o_ref.dtype)
    xx_ref[0:POOL_HALO, :] = xx_ref[tl:tl + POOL_HALO, :]


def _pool(z, buf16, pool_w, pool_scale, l, *, tl, start, out_dtype):
    b, t, _ = z.shape
    w = buf16.shape[2]
    assert w == len(POOL_WINDOWS) * LANE
    return pl.pallas_call(
        functools.partial(_pool_kernel, tl=tl, start=start),
        grid=(b, t // tl),
        in_specs=[
            pl.BlockSpec((None, tl, w), lambda bi, ti: (bi, ti, 0)),
            pl.BlockSpec((None, POOL_HALO, w), lambda bi, ti: (bi, 0, 0)),
            pl.BlockSpec((None, len(POOL_WINDOWS), LANE, LANE), lambda bi, ti: (l, 0, 0, 0)),
            pl.BlockSpec((None, 1, w), lambda bi, ti: (l, 0, 0)),
        ],
        out_specs=pl.BlockSpec((None, tl, w), lambda bi, ti: (bi, ti, 0)),
        out_shape=jax.ShapeDtypeStruct((b, t, w), out_dtype),
        scratch_shapes=[pltpu.VMEM((POOL_HALO + tl, w), F32)],
        compiler_params=_params(("parallel", "arbitrary")),
        name="pool",
    )(z, buf16, pool_w, pool_scale)


def _s5_kernel(u_ref, s0r_ref, s0i_ref, lr_ref, li_ref, ldt_ref, br_ref, bi_ref, cr_ref, ci_ref,
               d_ref, wg_ref, bg_ref, y_ref, sr_ref, si_ref,
               pr_ref, pi_ref, qr_ref, qi_ref, apr_ref, api_ref, zr_ref, zi_ref, car_ref, cai_ref,
               *, tl, pad, n_chunk):
    ti = pl.program_id(1)
    n_state = lr_ref.shape[1]
    n_pass = int(math.log2(tl))
    uc = u_ref.shape[1] // n_chunk
    sc = n_state // n_chunk

    @pl.when(ti == 0)
    def _():
        dt = jnp.exp(ldt_ref[...])
        lr, li = lr_ref[...], li_ref[...]
        mag = jnp.exp(lr * dt)
        a_re, a_im = mag * jnp.cos(li * dt), mag * jnp.sin(li * dt)
        den = lr * lr + li * li
        zr_ref[...] = ((a_re - 1.0) * lr + a_im * li) / den
        zi_ref[...] = (a_im * lr - (a_re - 1.0) * li) / den
        pw_re, pw_im = a_re, a_im
        for j in range(n_pass):
            apr_ref[j:j + 1, :] = pw_re
            api_ref[j:j + 1, :] = pw_im
            pw_re, pw_im = pw_re * pw_re - pw_im * pw_im, 2.0 * (pw_re * pw_im)
        car_ref[...] = s0r_ref[...]
        cai_ref[...] = s0i_ref[...]
        zeros = jnp.zeros((pad, n_state), F32)
        pr_ref[0:pad, :] = zeros
        pi_ref[0:pad, :] = zeros
        qr_ref[0:pad, :] = zeros
        qi_ref[0:pad, :] = zeros

    u = u_ref[...]
    for c in range(n_chunk):
        uu = u[:, c * uc:(c + 1) * uc]
        ub_re = _dot(uu, br_ref[c])
        ub_im = _dot(uu, bi_ref[c])
        cols = slice(c * sc, (c + 1) * sc)
        z_re, z_im = zr_ref[:, cols], zi_ref[:, cols]
        pr_ref[pad:pad + tl, cols] = z_re * ub_re - z_im * ub_im
        pi_ref[pad:pad + tl, cols] = z_re * ub_im + z_im * ub_re
    a_re, a_im = apr_ref[0:1, :], api_ref[0:1, :]
    c_re, c_im = car_ref[...], cai_ref[...]
    pr_ref[pad:pad + 1, :] += a_re * c_re - a_im * c_im
    pi_ref[pad:pad + 1, :] += a_re * c_im + a_im * c_re

    bufs = ((pr_ref, pi_ref), (qr_ref, qi_ref))
    for j in range(n_pass):
        k = 1 << j
        (s_re, s_im), (d_re, d_im) = bufs[j % 2], bufs[(j + 1) % 2]

        def col_body(cb, carry, k=k, j=j, s_re=s_re, s_im=s_im, d_re=d_re, d_im=d_im):
            for sub in range(SCAN_COLS):
                cols = pl.ds(pl.multiple_of((cb * SCAN_COLS + sub) * LANE, LANE), LANE)
                ar, ai = apr_ref[j:j + 1, cols], api_ref[j:j + 1, cols]
                cur_re, cur_im = s_re[pad:pad + tl, cols], s_im[pad:pad + tl, cols]
                sh_re, sh_im = s_re[pad - k:pad - k + tl, cols], s_im[pad - k:pad - k + tl, cols]
                d_re[pad:pad + tl, cols] = cur_re + (ar * sh_re - ai * sh_im)
                d_im[pad:pad + tl, cols] = cur_im + (ar * sh_im + ai * sh_re)
            return carry

        lax.fori_loop(0, n_state // (SCAN_COLS * LANE), col_body, 0)
    f_re, f_im = bufs[n_pass % 2]

    car_ref[...] = f_re[pad + tl - 1:pad + tl, :]
    cai_ref[...] = f_im[pad + tl - 1:pad + tl, :]
    sr_ref[...] = car_ref[...]
    si_ref[...] = cai_ref[...]

    ys = []
    for c in range(n_chunk):
        cols = slice(c * sc, (c + 1) * sc)
        ys.append(_dot(f_re[pad:pad + tl, cols], cr_ref[c]) - _dot(f_im[pad:pad + tl, cols], ci_ref[c]))
    y = jnp.concatenate(ys, axis=1) + d_ref[...] * u
    y = 0.5 * y * (1.0 + jnp.tanh(math.sqrt(2.0 / math.pi) * (y + 0.044715 * (y * y * y))))
    y_ref[...] = (y * _sigmoid(_dot(y, wg_ref[...]) + bg_ref[...])).astype(y_ref.dtype)


def _s5_seg_kernel(u_ref, s0r_ref, s0i_ref, lr_ref, li_ref, ldt_ref, br_ref, bi_ref, cr_ref, ci_ref,
                   d_ref, wg_ref, bg_ref, y_ref, sr_ref, si_ref,
                   pr_ref, pi_ref, qr_ref, qi_ref, apr_ref, api_ref, zr_ref, zi_ref, car_ref, cai_ref,
                   axr_ref, axi_ref, e0r_ref, e0i_ref, e1r_ref, e1i_ref, pm_ref, pmt_ref,
                   *, tl, pad, n_chunk):
    ti = pl.program_id(1)
    n_state = lr_ref.shape[1]
    seg_len = tl // SUBLANES
    n_local = int(math.log2(seg_len))
    n_link = int(math.log2(SUBLANES))
    uc = u_ref.shape[1] // n_chunk
    sc = n_state // n_chunk

    def cmul(ar, ai, br, bi):
        return ar * br - ai * bi, ar * bi + ai * br

    @pl.when(ti == 0)
    def _():
        dt = jnp.exp(ldt_ref[...])
        lr, li = lr_ref[...], li_ref[...]
        mag = jnp.exp(lr * dt)
        a_re, a_im = mag * jnp.cos(li * dt), mag * jnp.sin(li * dt)
        den = lr * lr + li * li
        zr_ref[...] = ((a_re - 1.0) * lr + a_im * li) / den
        zi_ref[...] = (a_im * lr - (a_re - 1.0) * li) / den
        pw_re, pw_im = a_re, a_im
        for j in range(n_local + n_link):
            apr_ref[j:j + 1, :] = pw_re
            api_ref[j:j + 1, :] = pw_im
            pw_re, pw_im = pw_re * pw_re - pw_im * pw_im, 2.0 * (pw_re * pw_im)
        axr_ref[0:SUBLANES, :] = jnp.broadcast_to(a_re, (SUBLANES, n_state))
        axi_ref[0:SUBLANES, :] = jnp.broadcast_to(a_im, (SUBLANES, n_state))
        for j in range(n_local):
            n_rows = SUBLANES << j
            nr, ni = cmul(axr_ref[0:n_rows, :], axi_ref[0:n_rows, :], apr_ref[j:j + 1, :], api_ref[j:j + 1, :])
            axr_ref[n_rows:2 * n_rows, :] = nr
            axi_ref[n_rows:2 * n_rows, :] = ni
        car_ref[...] = s0r_ref[...]
        cai_ref[...] = s0i_ref[...]
        zeros = jnp.zeros((pad, n_state), F32)
        for ref in (pr_ref, pi_ref, qr_ref, qi_ref):
            ref[0:pad, :] = zeros
        for ref in (e0r_ref, e0i_ref, e1r_ref, e1i_ref):
            ref[0:SUBLANES, :] = jnp.zeros((SUBLANES, n_state), F32)
        i0 = lax.broadcasted_iota(jnp.int32, (tl, tl), 0)
        i1 = lax.broadcasted_iota(jnp.int32, (tl, tl), 1)
        dest = lambda t: jnp.left_shift(jnp.bitwise_and(t, seg_len - 1), n_link) + jnp.right_shift(t, n_local)
        pm_ref[...] = jnp.where(i1 == dest(i0), 1.0, 0.0).astype(BF16)
        pmt_ref[...] = jnp.where(i0 == dest(i1), 1.0, 0.0).astype(BF16)

    u = u_ref[...]
    up = jnp.dot(pmt_ref[...], u.astype(BF16), preferred_element_type=F32).astype(BF16)
    for c in range(n_chunk):
        uu = up[:, c * uc:(c + 1) * uc]
        ub_re = _dot(uu, br_ref[c])
        ub_im = _dot(uu, bi_ref[c])
        cols = slice(c * sc, (c + 1) * sc)
        z_re, z_im = zr_ref[:, cols], zi_ref[:, cols]
        pr_ref[pad:pad + tl, cols] = z_re * ub_re - z_im * ub_im
        pi_ref[pad:pad + tl, cols] = z_re * ub_im + z_im * ub_re
    add_re, add_im = cmul(apr_ref[0:1, :], api_ref[0:1, :], car_ref[...], cai_ref[...])
    pr_ref[pad:pad + 1, :] += add_re
    pi_ref[pad:pad + 1, :] += add_im

    bufs = ((pr_ref, pi_ref), (qr_ref, qi_ref))
    for j in range(n_local):
        k = SUBLANES << j
        (s_re, s_im), (d_re, d_im) = bufs[j % 2], bufs[(j + 1) % 2]

        def col_body(cb, carry, k=k, j=j, s_re=s_re, s_im=s_im, d_re=d_re, d_im=d_im):
            for sub in range(SCAN_COLS):
                cols = pl.ds(pl.multiple_of((cb * SCAN_COLS + sub) * LANE, LANE), LANE)
                mr, mi = cmul(apr_ref[j:j + 1, cols], api_ref[j:j + 1, cols],
                              s_re[pad - k:pad - k + tl, cols], s_im[pad - k:pad - k + tl, cols])
                d_re[pad:pad + tl, cols] = s_re[pad:pad + tl, cols] + mr
                d_im[pad:pad + tl, cols] = s_im[pad:pad + tl, cols] + mi
            return carry

        lax.fori_loop(0, n_state // (SCAN_COLS * LANE), col_body, 0)
    f_re, f_im = bufs[n_local % 2]

    e_bufs = ((e0r_ref, e0i_ref), (e1r_ref, e1i_ref))
    lo, hi = SUBLANES, 2 * SUBLANES
    e0r_ref[lo:hi, :] = f_re[pad + tl - SUBLANES:pad + tl, :]
    e0i_ref[lo:hi, :] = f_im[pad + tl - SUBLANES:pad + tl, :]
    for jj in range(n_link):
        k = 1 << jj
        (s_re, s_im), (d_re, d_im) = e_bufs[jj % 2], e_bufs[(jj + 1) % 2]
        mr, mi = cmul(apr_ref[n_local + jj:n_local + jj + 1, :], api_ref[n_local + jj:n_local + jj + 1, :],
                      s_re[lo - k:hi - k, :], s_im[lo - k:hi - k, :])
        d_re[lo:hi, :] = s_re[lo:hi, :] + mr
        d_im[lo:hi, :] = s_im[lo:hi, :] + mi
    er_ref, ei_ref = e_bufs[n_link % 2]
    car_ref[...] = er_ref[hi - 1:hi, :]
    cai_ref[...] = ei_ref[hi - 1:hi, :]
    sr_ref[...] = car_ref[...]
    si_ref[...] = cai_ref[...]

    def apply_body(cb, carry):
        for sub in range(SCAN_COLS):
            cols = pl.ds(pl.multiple_of((cb * SCAN_COLS + sub) * LANE, LANE), LANE)
            pe_re, pe_im = er_ref[lo - 1:hi - 1, cols], ei_ref[lo - 1:hi - 1, cols]
            for i in range(seg_len):
                rows = slice(i * SUBLANES, (i + 1) * SUBLANES)
                dst = slice(pad + i * SUBLANES, pad + (i + 1) * SUBLANES)
                mr, mi = cmul(axr_ref[rows, cols], axi_ref[rows, cols], pe_re, pe_im)
                f_re[dst, cols] += mr
                f_im[dst, cols] += mi
        return carry

    lax.fori_loop(0, n_state // (SCAN_COLS * LANE), apply_body, 0)

    ys = []
    for c in range(n_chunk):
        cols = slice(c * sc, (c + 1) * sc)
        ys.append(_dot(f_re[pad:pad + tl, cols], cr_ref[c]) - _dot(f_im[pad:pad + tl, cols], ci_ref[c]))
    y_hi, y_lo = _split(jnp.concatenate(ys, axis=1))
    pm = pm_ref[...]
    y = (jnp.dot(pm, y_hi, preferred_element_type=F32) + jnp.dot(pm, y_lo, preferred_element_type=F32)
         + d_ref[...] * u)
    y = 0.5 * y * (1.0 + jnp.tanh(math.sqrt(2.0 / math.pi) * (y + 0.044715 * (y * y * y))))
    y_ref[...] = (y * _sigmoid(_dot(y, wg_ref[...]) + bg_ref[...])).astype(y_ref.dtype)


def _s5(z, col0, s0_re, s0_im, lam_re, lam_im, ldt_rep, b_re_c, b_im_c, c_re_c, c_im_c, ssm_d, glu_w, glu_b, l,
        *, tl, out_dtype):
    b, t, _ = z.shape
    n_state = s0_re.shape[2]
    n_chunk, uc, sc = b_re_c.shape[1:]
    u_w = n_chunk * uc
    assert col0 % u_w == 0 and n_chunk * sc == n_state
    pad = max(SUBLANES, tl // 2)
    n_pass = int(math.log2(tl))
    assert 1 << n_pass == tl
    vec = lambda w: pl.BlockSpec((None, 1, w), lambda bi, ti: (l, 0, 0))
    state_spec = pl.BlockSpec((None, 1, n_state), lambda bi, ti: (bi, 0, 0))
    scratch = ([pltpu.VMEM((pad + tl, n_state), F32)] * 4
               + [pltpu.VMEM((max(SUBLANES, n_pass), n_state), F32)] * 2
               + [pltpu.VMEM((1, n_state), F32)] * 4)
    segmented = tl >= SUBLANES * SUBLANES
    if segmented:
        scratch += ([pltpu.VMEM((tl, n_state), F32)] * 2 + [pltpu.VMEM((2 * SUBLANES, n_state), F32)] * 4
                    + [pltpu.VMEM((tl, tl), BF16)] * 2)
    return pl.pallas_call(
        functools.partial(_s5_seg_kernel if segmented else _s5_kernel, tl=tl, pad=pad, n_chunk=n_chunk),
        grid=(b, t // tl),
        in_specs=[
            pl.BlockSpec((None, tl, u_w), lambda bi, ti: (bi, ti, col0 // u_w)),
            state_spec, state_spec,
            vec(n_state), vec(n_state), vec(n_state),
            pl.BlockSpec((None, n_chunk, uc, sc), lambda bi, ti: (l, 0, 0, 0)),
            pl.BlockSpec((None, n_chunk, uc, sc), lambda bi, ti: (l, 0, 0, 0)),
            pl.BlockSpec((None, n_chunk, sc, uc), lambda bi, ti: (l, 0, 0, 0)),
            pl.BlockSpec((None, n_chunk, sc, uc), lambda bi, ti: (l, 0, 0, 0)),
            vec(u_w),
            pl.BlockSpec((None, u_w, u_w), lambda bi, ti: (l, 0, 0)),
            vec(u_w),
        ],
        out_specs=[
            pl.BlockSpec((None, tl, u_w), lambda bi, ti: (bi, ti, 0)),
            state_spec, state_spec,
        ],
        out_shape=[
            jax.ShapeDtypeStruct((b, t, u_w), out_dtype),
            jax.ShapeDtypeStruct((b, 1, n_state), F32),
            jax.ShapeDtypeStruct((b, 1, n_state), F32),
        ],
        scratch_shapes=scratch,
        compiler_params=_params(("parallel", "arbitrary")),
        name="s5",
    )(z, s0_re, s0_im, lam_re, lam_im, ldt_rep, b_re_c, b_im_c, c_re_c, c_im_c, ssm_d, glu_w, glu_b)


def _block_diag_chunks(w, n_chunk):
    nl, g, r, c = w.shape
    gc = g // n_chunk
    w = w.reshape(nl, n_chunk, gc, r, c)
    eye = jnp.eye(gc, dtype=w.dtype)
    out = w[:, :, :, :, None, :] * eye[None, None, :, None, :, None]
    return out.reshape(nl, n_chunk, gc * r, gc * c)


def _top_blocks(gate, n_valid_col):
    lane = lax.broadcasted_iota(jnp.int32, gate.shape, 1)
    lane_f = lane.astype(F32)
    gate = jnp.where(lane < n_valid_col, gate, NEG_INF)
    sel = jnp.zeros(gate.shape, F32)
    for _ in range(MOBA_TOPK):
        m = jnp.max(gate, axis=1, keepdims=True)
        idx = jnp.min(jnp.where(gate == m, lane_f, float(gate.shape[1])), axis=1, keepdims=True)
        hit = lane_f == idx
        valid = jnp.broadcast_to(m, gate.shape) > 0.5 * NEG_INF
        sel = jnp.where(hit & valid, 1.0, sel)
        gate = jnp.where(hit, NEG_INF, gate)
    return sel


def _pick_column(sel_b, j):
    k = lax.broadcasted_iota(jnp.int32, (LANE, LANE), 0)
    onehot = jnp.where(k == j, 1.0, 0.0).astype(BF16)
    return jnp.dot(sel_b, onehot, preferred_element_type=F32) > 0.5


def _lane_tile(x, k):
    return x if k == 1 else jnp.concatenate([x] * k, axis=1)


def _moba_prompt_kernel(slope_ref, q_ref, k_ref, v_ref, o_ref, kb_ref, vb_ref, km_ref, *, n_blk, hp, cb, scale):
    hg = pl.program_id(1)
    n = pl.program_id(2)
    blk = MOBA_BLOCK
    wc = cb * blk
    rows = hp * blk
    t = k_ref.shape[0]
    stack = lambda per_head: jnp.concatenate(per_head, axis=0)
    head_rows = lambda x, hh: x[hh * blk:(hh + 1) * blk]

    @pl.when(n == 0)
    def _():
        km_ref[...] = jnp.zeros_like(km_ref)
        for hh in range(hp):
            cols = slice(hh * HEAD_DIM, (hh + 1) * HEAD_DIM)
            kb_ref[hh] = k_ref[:, cols].astype(BF16)
            vb_ref[hh, :, 0:HEAD_DIM] = v_ref[:, cols].astype(BF16)
            vb_ref[hh, :, HEAD_DIM:] = jnp.ones((t, HEAD_DIM), BF16)
            for j in range(n_blk):
                km_ref[hh, j:j + 1, :] = jnp.mean(k_ref[j * blk:(j + 1) * blk, cols], axis=0, keepdims=True)

    slopes = [slope_ref[hg * hp + hh] for hh in range(hp)]
    cpos = lax.broadcasted_iota(jnp.int32, (1, wc), 1).astype(F32)
    col_bias = [s * cpos for s in slopes]
    slope_rows = stack([jnp.zeros((blk, LANE), F32) + s for s in slopes])
    qs = [q_ref[:, hh * HEAD_DIM:(hh + 1) * HEAD_DIM] for hh in range(hp)]
    qbs = [(q * scale).astype(BF16) for q in qs]
    gate = stack([_dot_nt(q, km_ref[hh]) for hh, q in enumerate(qs)])
    sel_b = _top_blocks(gate, n).astype(BF16)
    rowmax = lambda x: jnp.broadcast_to(jnp.max(x, axis=1, keepdims=True), (rows, LANE))

    own0 = pl.multiple_of(n * blk, blk)
    causal = stack([lax.broadcasted_iota(jnp.int32, (blk, blk), 0)
                    >= lax.broadcasted_iota(jnp.int32, (blk, blk), 1)] * hp)
    s = stack([_dot_nt(qb, kb_ref[hh, pl.ds(own0, blk), :]) + col_bias[hh][:, :blk] for hh, qb in enumerate(qbs)])
    s = jnp.where(causal, s, NEG_INF)
    m0 = rowmax(s)
    p = jnp.exp(s - _lane_tile(m0, blk // LANE))
    acc0 = stack([_dot(head_rows(p, hh), vb_ref[hh, pl.ds(own0, blk), :]) for hh in range(hp)])

    def body(c, carry):
        m, acc = carry
        c0 = pl.multiple_of(c * wc, wc)
        shift = slope_rows * (n * blk - c * wc).astype(F32)
        s = stack([_dot_nt(qb, kb_ref[hh, pl.ds(c0, wc), :]) + col_bias[hh] for hh, qb in enumerate(qbs)])
        groups = [s[:, g * blk:(g + 1) * blk] for g in range(cb)]
        chosen = [_pick_column(sel_b, c * cb + g) for g in range(cb)]
        m_new = m
        for g in range(cb):
            m_new = jnp.maximum(m_new, jnp.where(chosen[g], rowmax(groups[g]) - shift, NEG_INF))
        p = jnp.concatenate(
            [jnp.exp(groups[g] - _lane_tile(jnp.where(chosen[g], m_new + shift, -NEG_INF), blk // LANE))
             for g in range(cb)], axis=1)
        pv = stack([_dot(head_rows(p, hh), vb_ref[hh, pl.ds(c0, wc), :]) for hh in range(hp)])
        return m_new, _lane_tile(jnp.exp(m - m_new), 2 * HEAD_DIM // LANE) * acc + pv

    _, acc = lax.fori_loop(0, (n + cb - 1) // cb, body, (m0, acc0))
    for hh in range(hp):
        a = head_rows(acc, hh)
        o_ref[:, hh * HEAD_DIM:(hh + 1) * HEAD_DIM] = (a[:, :HEAD_DIM] / a[:, HEAD_DIM:]).astype(o_ref.dtype)


def _moba_prompt(z, slopes, q_col, k_col, v_col, n_heads, *, hp, cb):
    b, t, _ = z.shape
    n_blk = t // MOBA_BLOCK
    w = hp * HEAD_DIM
    assert n_blk * MOBA_BLOCK == t and n_blk <= LANE and n_heads % hp == 0 and n_blk % cb == 0
    assert q_col % w == 0 and k_col % w == 0 and v_col % w == 0
    qc, kc, vc = q_col // w, k_col // w, v_col // w
    whole = lambda c: pl.BlockSpec((None, t, w), lambda bi, hg, n, s: (bi, 0, c + hg),
                                   pipeline_mode=pl.Buffered(1))
    return pl.pallas_call(
        functools.partial(_moba_prompt_kernel, n_blk=n_blk, hp=hp, cb=cb, scale=HEAD_DIM ** -0.5),
        grid_spec=pltpu.PrefetchScalarGridSpec(
            num_scalar_prefetch=1,
            grid=(b, n_heads // hp, n_blk),
            in_specs=[
                pl.BlockSpec((None, MOBA_BLOCK, w), lambda bi, hg, n, s: (bi, n, qc + hg)),
                whole(kc), whole(vc),
            ],
            out_specs=pl.BlockSpec((None, MOBA_BLOCK, w), lambda bi, hg, n, s: (bi, n, hg)),
            scratch_shapes=[pltpu.VMEM((hp, t, HEAD_DIM), BF16), pltpu.VMEM((hp, t, 2 * HEAD_DIM), BF16),
                            pltpu.VMEM((hp, LANE, HEAD_DIM), F32)],
        ),
        out_shape=jax.ShapeDtypeStruct((b, t, n_heads * HEAD_DIM), BF16),
        compiler_params=_params(("parallel", "parallel", "arbitrary")),
        name="moba_prompt",
    )(slopes, z, z, z)


def _page_spec(n_heads, index_map):
    return pl.BlockSpec((None, None, PAGE_SIZE * n_heads, HEAD_DIM), index_map)


def _kmean_kernel(pt_ref, *refs, n_heads, bps):
    pages, o_ref = refs[:PAGES_PER_BLOCK * bps], refs[PAGES_PER_BLOCK * bps]
    jj = pl.program_id(2)
    for bi in range(bps):
        tot = jnp.zeros((n_heads, HEAD_DIM), F32)
        for pg in range(PAGES_PER_BLOCK):
            page = pages[PAGES_PER_BLOCK * bi + pg][...].reshape(PAGE_SIZE, n_heads, HEAD_DIM)
            tot = tot + jnp.sum(page, axis=0)
        row0 = pl.multiple_of((jj * bps + bi) * n_heads, n_heads)
        o_ref[pl.ds(row0, n_heads), :] = tot * (1.0 / MOBA_BLOCK)


def _paged_block_means(cache, page_table, n_heads, *, bps):
    nl = cache.shape[0]
    b, n_pages = page_table.shape
    n_blk = n_pages // PAGES_PER_BLOCK
    assert n_blk % bps == 0

    def spec(k):
        return _page_spec(n_heads, lambda l, bi, jj, pt: (l, pt[bi, PAGES_PER_BLOCK * bps * jj + k], 0, 0))

    return pl.pallas_call(
        functools.partial(_kmean_kernel, n_heads=n_heads, bps=bps),
        grid_spec=pltpu.PrefetchScalarGridSpec(
            num_scalar_prefetch=1,
            grid=(nl, b, n_blk // bps),
            in_specs=[spec(k) for k in range(PAGES_PER_BLOCK * bps)],
            out_specs=pl.BlockSpec((None, None, n_blk * n_heads, HEAD_DIM), lambda l, bi, jj, pt: (l, bi, 0, 0)),
        ),
        out_shape=jax.ShapeDtypeStruct((nl, b, n_blk * n_heads, HEAD_DIM), F32),
        compiler_params=_params(("parallel", "parallel", "arbitrary")),
        name="paged_block_means",
    )(page_table, *([cache] * (PAGES_PER_BLOCK * bps)))


def _head_major(page_ref, n_heads):
    return jnp.concatenate([page_ref[pl.ds(h, PAGE_SIZE, stride=n_heads), :] for h in range(n_heads)], axis=1)


def _moba_sample_kernel(pt_ref, slope_ref, q_ref, kn_ref, vn_ref, km_ref, *refs,
                        n_heads, n_blk, bps, past_len, scale):
    npg = PAGES_PER_BLOCK * bps
    k_refs, v_refs = refs[:npg], refs[npg:2 * npg]
    o_ref, qbd_ref, sel_ref, slp_ref, m_ref, l_ref, acc_ref = refs[2 * npg:]
    jj = pl.program_id(1)
    t, width = q_ref.shape
    rows = n_heads * t
    rid = lax.broadcasted_iota(jnp.int32, (rows, 1), 0).astype(F32)
    hq = jnp.floor((rid + 0.5) * (1.0 / t))
    tq = rid - t * hq

    @pl.when(jj == 0)
    def _():
        q = q_ref[...]
        c = lax.broadcasted_iota(jnp.int32, (t, width), 1)
        per_head = [jnp.where((c >= h * HEAD_DIM) & (c < (h + 1) * HEAD_DIM), q, 0.0) for h in range(n_heads)]
        qbd_ref[...] = (jnp.concatenate(per_head, axis=0) * scale).astype(BF16)
        gates = []
        for h in range(n_heads):
            km_h = km_ref[pl.ds(h, n_blk, stride=n_heads), :]
            km_h = jnp.concatenate([km_h, jnp.zeros((LANE - n_blk, HEAD_DIM), F32)], axis=0)
            gates.append(_dot_nt(q[:, h * HEAD_DIM:(h + 1) * HEAD_DIM], km_h))
        sel_ref[...] = _top_blocks(jnp.concatenate(gates, axis=0), n_blk).astype(BF16)
        slp = jnp.zeros((rows, 1), F32)
        for h in range(n_heads):
            slp = jnp.where(hq == h, slope_ref[h], slp)
        slp_ref[...] = slp
        fill = jnp.zeros((LANE - t, width), F32)
        s = _dot_nt(qbd_ref[...], jnp.concatenate([kn_ref[...], fill], axis=0))
        cpos = lax.broadcasted_iota(jnp.int32, (rows, LANE), 1).astype(F32)
        s = jnp.where(cpos <= tq, s - slp * (tq - cpos), NEG_INF)
        m0 = jnp.max(s, axis=1, keepdims=True)
        p = jnp.exp(s - m0)
        m_ref[...] = m0
        l_ref[...] = jnp.sum(p, axis=1, keepdims=True)
        acc_ref[...] = _dot(p, jnp.concatenate([vn_ref[...], fill], axis=0))

    slp = slp_ref[...]
    qbd = qbd_ref[...]
    sel_b = sel_ref[...]
    kmat = jnp.concatenate([_head_major(r, n_heads) for r in k_refs], axis=0).astype(BF16)
    vmat = jnp.concatenate([_head_major(r, n_heads) for r in v_refs], axis=0).astype(BF16)
    kpos = lax.broadcasted_iota(jnp.int32, (rows, bps * MOBA_BLOCK), 1).astype(F32)
    s = _dot_nt(qbd, kmat) - slp * (tq - kpos)
    shift = slp * (past_len - jj * (bps * MOBA_BLOCK)).astype(F32)
    groups = [s[:, g * MOBA_BLOCK:(g + 1) * MOBA_BLOCK] for g in range(bps)]
    chosen = [_pick_column(sel_b, jj * bps + g)[:, 0:1] for g in range(bps)]
    m = m_ref[...]
    m_new = m
    for g in range(bps):
        m_new = jnp.maximum(m_new, jnp.where(chosen[g], jnp.max(groups[g], axis=1, keepdims=True) - shift, NEG_INF))
    p = jnp.concatenate([jnp.exp(groups[g] - jnp.where(chosen[g], m_new + shift, -NEG_INF)) for g in range(bps)],
                        axis=1)
    corr = jnp.exp(m - m_new)
    lsum = corr * l_ref[...] + jnp.sum(p, axis=1, keepdims=True)
    acc = corr * acc_ref[...] + _dot(p, vmat)
    m_ref[...], l_ref[...], acc_ref[...] = m_new, lsum, acc

    @pl.when(jj == pl.num_programs(1) - 1)
    def _():
        out = acc / lsum
        o_ref[...] = jnp.concatenate(
            [out[h * t:(h + 1) * t, h * HEAD_DIM:(h + 1) * HEAD_DIM] for h in range(n_heads)], axis=1)


def _moba_sample(z, kmeans, slopes, cache_k, cache_v, page_table, l, q_col, k_col, v_col, n_heads, *, bps):
    b, t, _ = z.shape
    n_pages = page_table.shape[1]
    n_blk = n_pages // PAGES_PER_BLOCK
    width = n_heads * HEAD_DIM
    npg = PAGES_PER_BLOCK * bps
    rows = n_heads * t
    assert n_blk % bps == 0 and n_blk <= LANE and t <= LANE and rows % 16 == 0
    assert q_col % width == 0 and k_col % width == 0 and v_col % width == 0

    def page(k):
        return _page_spec(n_heads, lambda bi, jj, pt, sl: (l, pt[bi, npg * jj + k], 0, 0))

    tok = lambda col: pl.BlockSpec((None, t, width), lambda bi, jj, pt, sl: (bi, 0, col // width))
    return pl.pallas_call(
        functools.partial(_moba_sample_kernel, n_heads=n_heads, n_blk=n_blk, bps=bps,
                          past_len=n_pages * PAGE_SIZE, scale=HEAD_DIM ** -0.5),
        grid_spec=pltpu.PrefetchScalarGridSpec(
            num_scalar_prefetch=2,
            grid=(b, n_blk // bps),
            in_specs=[tok(q_col), tok(k_col), tok(v_col),
                      pl.BlockSpec((None, None, n_blk * n_heads, HEAD_DIM), lambda bi, jj, pt, sl: (l, bi, 0, 0))]
            + [page(k) for k in range(npg)] + [page(k) for k in range(npg)],
            out_specs=pl.BlockSpec((None, t, width), lambda bi, jj, pt, sl: (bi, 0, 0)),
            scratch_shapes=[pltpu.VMEM((rows, width), BF16), pltpu.VMEM((rows, LANE), BF16),
                            pltpu.VMEM((rows, 1), F32), pltpu.VMEM((rows, 1), F32), pltpu.VMEM((rows, 1), F32),
                            pltpu.VMEM((rows, width), F32)],
        ),
        out_shape=jax.ShapeDtypeStruct((b, t, width), F32),
        compiler_params=_params(("parallel", "arbitrary")),
        name="moba_sample",
    )(page_table, slopes, z, z, z, kmeans, *([cache_k] * npg), *([cache_v] * npg))


def _merge_ln_kernel(a_ref, s_ref, c_ref, g0_ref, g1_ref, g2_ref, bg0_ref, bg1_ref, bg2_ref,
                     wp_ref, ws_ref, wa_ref, wo_ref, h_ref, g_ref, b_ref, o_ref, *, alpha):
    j = pl.program_id(1)

    @pl.when(j == 0)
    def _():
        o_ref[...] = jnp.zeros_like(o_ref)

    gated = lambda g_ref, bg_ref, x_ref, w_ref: (
        (jnp.tanh(0.5 * (g_ref[...] + bg_ref[...])) + 1.0) * _dot(x_ref[...], w_ref[...]))
    merged = 0.5 * (gated(g0_ref, bg0_ref, a_ref, wp_ref) + gated(g1_ref, bg1_ref, s_ref, ws_ref)
                    + gated(g2_ref, bg2_ref, c_ref, wa_ref))
    o_ref[...] += _dot(merged, wo_ref[...])

    @pl.when(j == pl.num_programs(1) - 1)
    def _():
        o_ref[...] = _layer_norm(alpha * h_ref[...] + o_ref[...], g_ref[...], b_ref[...])


def _merge_ln(a_out, s_out, c_out, z, gate_col, b_gate, wp, ws, wa, wo, h, ln_g, ln_b, l, ln_idx,
              *, alpha, tm, tn):
    m, d = h.shape
    gc = gate_col // tn
    nj = d // tn
    row = lambda w: pl.BlockSpec((tm, w), lambda i, j: (i, 0), pipeline_mode=pl.Buffered(1))
    gate = lambda g: pl.BlockSpec((tm, tn), lambda i, j: (i, gc + g * nj + j))
    bias = lambda g: pl.BlockSpec((None, 1, tn), lambda i, j: (l, 0, g * nj + j))
    wcol = lambda k: pl.BlockSpec((None, k, tn), lambda i, j: (l, 0, j))
    return pl.pallas_call(
        functools.partial(_merge_ln_kernel, alpha=alpha),
        grid=(m // tm, nj),
        in_specs=[
            row(a_out.shape[1]), row(s_out.shape[1]), row(c_out.shape[1]),
            gate(0), gate(1), gate(2), bias(0), bias(1), bias(2),
            wcol(a_out.shape[1]), wcol(s_out.shape[1]), wcol(c_out.shape[1]),
            pl.BlockSpec((None, tn, d), lambda i, j: (l, j, 0)),
            row(d),
            pl.BlockSpec((None, 1, d), lambda i, j: (ln_idx, 0, 0)),
            pl.BlockSpec((None, 1, d), lambda i, j: (ln_idx, 0, 0)),
        ],
        out_specs=pl.BlockSpec((tm, d), lambda i, j: (i, 0), pipeline_mode=pl.Buffered(1)),
        out_shape=jax.ShapeDtypeStruct((m, d), F32),
        compiler_params=_params(("parallel", "arbitrary")),
        name="merge_ln",
    )(a_out, s_out, c_out, z, z, z, b_gate, b_gate, b_gate, wp, ws, wa, wo, h, ln_g, ln_b)


def _memattn_ln_kernel(h_ref, wq_ref, mk_ref, mv_ref, wo_ref, g_ref, b_ref, o_ref, *, alpha, scale):
    hx = h_ref[...]
    q = _dot(hx, wq_ref[...])
    mk, mv = mk_ref[...], mv_ref[...]
    outs = []
    for hd in range(N_MEM_HEADS):
        cols = slice(hd * MEM_HEAD_DIM, (hd + 1) * MEM_HEAD_DIM)
        s = _dot_nt(q[:, cols], mk[:, cols]) * scale
        s = s - jnp.max(s, axis=1, keepdims=True)
        p = jnp.exp(s)
        p = p / jnp.sum(p, axis=1, keepdims=True)
        outs.append(_dot(p, mv[:, cols]))
    o = jnp.concatenate(outs, axis=1)
    o_ref[...] = _layer_norm(alpha * hx + _dot(o, wo_ref[...]), g_ref[...], b_ref[...])


def _memattn_ln(h, mem_k, mem_v, mem_l, wq, wo, ln_g, ln_b, l, ln_idx, *, alpha, tm):
    b, t, d = h.shape
    mlen, w = mem_k.shape[2:]
    mem_spec = pl.BlockSpec((None, None, mlen, w), lambda bi, ti: (mem_l, bi, 0, 0))
    return pl.pallas_call(
        functools.partial(_memattn_ln_kernel, alpha=alpha, scale=MEM_HEAD_DIM ** -0.5),
        grid=(b, t // tm),
        in_specs=[
            pl.BlockSpec((None, tm, d), lambda bi, ti: (bi, ti, 0)),
            pl.BlockSpec((None, d, w), lambda bi, ti: (l, 0, 0)),
            mem_spec, mem_spec,
            pl.BlockSpec((None, w, d), lambda bi, ti: (l, 0, 0)),
            pl.BlockSpec((None, 1, d), lambda bi, ti: (ln_idx, 0, 0)),
            pl.BlockSpec((None, 1, d), lambda bi, ti: (ln_idx, 0, 0)),
        ],
        out_specs=pl.BlockSpec((None, tm, d), lambda bi, ti: (bi, ti, 0)),
        out_shape=jax.ShapeDtypeStruct((b, t, d), F32),
        compiler_params=_params(("parallel", "parallel")),
        name="memattn_ln",
    )(h, wq, mem_k, mem_v, wo, ln_g, ln_b)


def kernel(x_prompt, x_sample, cache_k, cache_v, cache_mem_k, cache_mem_v, state_pool, state_ssm_re, state_ssm_im, page_table, mem_prompt, ln_g, ln_b, ffn1_gate, ffn1_up, ffn1_down, w_in, b_gate, pool_w, pool_scale, ssm_lam_re, ssm_lam_im, ssm_log_dt, ssm_b_re, ssm_b_im, ssm_c_re, ssm_c_im, ssm_d, ssm_glu_w, ssm_glu_b, w_proj_pool, w_proj_ssm, w_proj_attn, w_out, mem_wq, mem_wk, mem_wv, mem_wo, ffn2_gate, ffn2_up, ffn2_down):
    depth = ln_g.shape[0]
    bp, tp, d = x_prompt.shape
    bs, ts, _ = x_sample.shape
    pool_width = state_pool.shape[3]
    n_groups, n_state = state_ssm_re.shape[2:]
    ssm_width = n_groups * SSM_GROUP
    n_heads = cache_k.shape[3]
    attn_width = n_heads * HEAD_DIM
    n_pages = page_table.shape[1]
    past_len = n_pages * PAGE_SIZE
    n_past_blk = n_pages // PAGES_PER_BLOCK
    mem_len = mem_prompt.shape[1]
    mem_width = N_MEM_HEADS * MEM_HEAD_DIM
    alpha = (2 * depth) ** 0.25
    ssm_col = pool_width
    q_col = ssm_col + ssm_width
    k_col = q_col + attn_width
    v_col = k_col + attn_width
    gate_col = v_col + attn_width

    ln_g2 = ln_g.reshape(depth * 4, 1, d)
    ln_b2 = ln_b.reshape(depth * 4, 1, d)
    b_gate2 = b_gate.reshape(depth, 1, N_BRANCH * d)
    pool_scale2 = pool_scale.reshape(depth, 1, pool_width)
    n_chunk = 4
    lam_re2 = ssm_lam_re.reshape(depth, 1, n_groups * n_state)
    lam_im2 = ssm_lam_im.reshape(depth, 1, n_groups * n_state)
    ldt_rep = jnp.repeat(ssm_log_dt, n_state, axis=1).reshape(depth, 1, n_groups * n_state)
    b_re_c = _block_diag_chunks(jnp.swapaxes(ssm_b_re, 2, 3), n_chunk)
    b_im_c = _block_diag_chunks(jnp.swapaxes(ssm_b_im, 2, 3), n_chunk)
    c_re_c = _block_diag_chunks(jnp.swapaxes(ssm_c_re, 2, 3), n_chunk)
    c_im_c = _block_diag_chunks(jnp.swapaxes(ssm_c_im, 2, 3), n_chunk)
    ssm_d2 = ssm_d.reshape(depth, 1, ssm_width)
    glu_b2 = ssm_glu_b.reshape(depth, 1, ssm_width)
    slopes = jnp.exp2(-8.0 * jnp.arange(1, n_heads + 1, dtype=F32) / n_heads)
    cache_k4 = cache_k.reshape(cache_k.shape[0], cache_k.shape[1], PAGE_SIZE * n_heads, HEAD_DIM)
    cache_v4 = cache_v.reshape(cache_v.shape[0], cache_v.shape[1], PAGE_SIZE * n_heads, HEAD_DIM)
    cmem_k = cache_mem_k.reshape(depth, bs, mem_len, mem_width)
    cmem_v = cache_mem_v.reshape(depth, bs, mem_len, mem_width)
    mem_rows = mem_prompt.reshape(bp * mem_len, d)
    zero_pool = jnp.zeros((bp, POOL_HALO, pool_width), F32)
    zero_state = jnp.zeros((bp, 1, n_groups * n_state), F32)
    state_pool16 = jnp.pad(state_pool, ((0, 0), (0, 0), (POOL_HALO - POOL_BUF, 0), (0, 0)))
    s_re_in = state_ssm_re.reshape(depth, bs, 1, n_groups * n_state)
    s_im_in = state_ssm_im.reshape(depth, bs, 1, n_groups * n_state)

    merge_w = [w.astype(BF16) for w in (w_proj_pool, w_proj_ssm, w_proj_attn, w_out)]

    kmeans = _paged_block_means(cache_k4, page_table, n_heads, bps=4)

    hp = x_prompt.reshape(bp * tp, d)
    hs = x_sample.reshape(bs * ts, d)
    mp, ms = bp * tp, bs * ts
    outs = {k: [] for k in ("poolp", "rep", "imp", "mkp", "mvp", "ks", "vs", "pools", "res", "ims")}
    kv_stacks = None

    def mix_common(h, z3, a_out, s_out, c_out, l, tm):
        m = h.shape[0]
        return _merge_ln(a_out.reshape(m, -1), s_out.reshape(m, -1), c_out.reshape(m, -1),
                         z3.reshape(m, -1), gate_col, b_gate2, *merge_w,
                         h, ln_g2, ln_b2, l, 4 * l + 1, alpha=alpha, tm=tm, tn=512)

    for l in range(depth):
        mk_p = _matmul(mem_rows, mem_wk, l, tm=bp * mem_len, tn=mem_width)
        mv_p = _matmul(mem_rows, mem_wv, l, tm=bp * mem_len, tn=mem_width)
        hp, hs = _ffn_ln(hp, hs, ffn1_gate, ffn1_up, ffn1_down, ln_g2, ln_b2, l, 4 * l, alpha=alpha,
                         tm=1024, tf=256)
        zp, *kv_stacks = _in_proj(hp, w_in, l, k_col, v_col, attn_width, kv_stacks, tm=2048, tn=512)
        zp = zp.reshape(bp, tp, -1)
        a_out = _pool(zp, zero_pool, pool_w, pool_scale2, l, tl=256, start=0, out_dtype=BF16)
        s_out, re_p, im_p = _s5(zp, ssm_col, zero_state, zero_state, lam_re2, lam_im2, ldt_rep,
                                b_re_c, b_im_c, c_re_c, c_im_c, ssm_d2, ssm_glu_w, glu_b2, l, tl=256,
                                out_dtype=BF16)
        c_out = _moba_prompt(zp, slopes, q_col, k_col, v_col, n_heads, hp=4, cb=2)
        hp = mix_common(hp, zp, a_out, s_out, c_out, l, 1024)
        hp = _memattn_ln(hp.reshape(bp, tp, d), mk_p.reshape(1, bp, mem_len, mem_width),
                         mv_p.reshape(1, bp, mem_len, mem_width), 0, mem_wq, mem_wo, ln_g2, ln_b2, l, 4 * l + 2,
                         alpha=alpha, tm=512).reshape(mp, d)
        outs["poolp"].append(zp[:, tp - POOL_BUF:, :pool_width])
        outs["rep"].append(re_p.reshape(bp, n_groups, n_state))
        outs["imp"].append(im_p.reshape(bp, n_groups, n_state))
        outs["mkp"].append(mk_p.reshape(bp, mem_len, N_MEM_HEADS, MEM_HEAD_DIM))
        outs["mvp"].append(mv_p.reshape(bp, mem_len, N_MEM_HEADS, MEM_HEAD_DIM))

        zs = _matmul(hs, w_in, l, tm=ms, tn=512).reshape(bs, ts, -1)
        a_out = _pool(zs, state_pool16[l], pool_w, pool_scale2, l, tl=ts, start=past_len, out_dtype=F32)
        s_out, re_s, im_s = _s5(zs, ssm_col, s_re_in[l], s_im_in[l], lam_re2, lam_im2, ldt_rep,
                                b_re_c, b_im_c, c_re_c, c_im_c, ssm_d2, ssm_glu_w, glu_b2, l, tl=ts,
                                out_dtype=F32)
        c_out = _moba_sample(zs, kmeans, slopes, cache_k4, cache_v4, page_table, l, q_col, k_col, v_col,
                             n_heads, bps=4)
        hs = mix_common(hs, zs, a_out, s_out, c_out, l, ms)
        hs = _memattn_ln(hs.reshape(bs, ts, d), cmem_k, cmem_v, l, mem_wq, mem_wo, ln_g2, ln_b2, l, 4 * l + 2,
                         alpha=alpha, tm=ts).reshape(ms, d)
        hp, hs = _ffn_ln(hp, hs, ffn2_gate, ffn2_up, ffn2_down, ln_g2, ln_b2, l, 4 * l + 3, alpha=alpha,
                         tm=1024, tf=256)
        outs["ks"].append(zs[:, :, k_col:v_col].reshape(bs, ts, n_heads, HEAD_DIM))
        outs["vs"].append(zs[:, :, v_col:gate_col].reshape(bs, ts, n_heads, HEAD_DIM))
        outs["pools"].append(jnp.concatenate([state_pool[l], zs[:, :, :pool_width]], axis=1)[:, -POOL_BUF:])
        outs["res"].append(re_s.reshape(bs, n_groups, n_state))
        outs["ims"].append(im_s.reshape(bs, n_groups, n_state))

    st = lambda k: jnp.stack(outs[k])
    kp, vp = (s.reshape(depth, bp, tp, n_heads, HEAD_DIM) for s in kv_stacks)
    return (hp.reshape(bp, tp, d), hs.reshape(bs, ts, d),
            kp, vp, st("poolp"), st("rep"), st("imp"), st("mkp"), st("mvp"),
            st("ks"), st("vs"), st("pools"), st("res"), st("ims"))
```

```python
import functools
import math

import jax
import jax.numpy as jnp
from jax import lax
from jax.experimental import pallas as pl
from jax.experimental.pallas import tpu as pltpu

F32 = jnp.float32
BF16 = jnp.bfloat16

POOL_WINDOWS = (2, 4, 8, 16)
POOL_BUF = max(POOL_WINDOWS) - 1
POOL_HALO = 16
SSM_GROUP = 16
SSM_STATE = 64
HEAD_DIM = 128
MOBA_BLOCK = 256
MOBA_TOPK = 3
PAGE_SIZE = 128
PAGES_PER_BLOCK = MOBA_BLOCK // PAGE_SIZE
N_MEM_HEADS = 4
MEM_HEAD_DIM = 128
N_BRANCH = 3
LN_EPS = 1e-5
NEG_INF = -1e30
LANE = 128
SUBLANES = 8
SCAN_COLS = 4
VMEM_LIMIT = 56 * 1024 * 1024


def _params(sem, vmem=VMEM_LIMIT):
    return pltpu.CompilerParams(dimension_semantics=sem, vmem_limit_bytes=vmem)


def _dot(a, b):
    return jnp.dot(a.astype(BF16), b.astype(BF16), preferred_element_type=F32)


def _dot_nt(a, b):
    return lax.dot_general(a.astype(BF16), b.astype(BF16), (((1,), (1,)), ((), ())),
                           preferred_element_type=F32)


def _split(a):
    hi = a.astype(BF16)
    lo = (a - hi.astype(F32)).astype(BF16)
    return hi, lo


def _dot3(a, b):
    ah, al = _split(a)
    bh, bl = _split(b)
    d = functools.partial(jnp.dot, preferred_element_type=F32)
    return d(ah, bh) + (d(ah, bl) + d(al, bh))


def _dot3_nt(a, b):
    ah, al = _split(a)
    bh, bl = _split(b)
    d = functools.partial(lax.dot_general, dimension_numbers=(((1,), (1,)), ((), ())),
                          preferred_element_type=F32)
    return d(ah, bh) + (d(ah, bl) + d(al, bh))


def _layer_norm(y, g, b):
    mu = jnp.mean(y, axis=-1, keepdims=True)
    d = y - mu
    var = jnp.mean(d * d, axis=-1, keepdims=True)
    return d * lax.rsqrt(var + LN_EPS) * g + b


def _sigmoid(x):
    return 0.5 * jnp.tanh(0.5 * x) + 0.5


def _ffn_ln_kernel(pt_ref, x_ref, xs_ref, wg_ref, wu_ref, wd_ref, g_ref, b_ref, *refs,
                   alpha, n_heads, pps, job_steps, steps_per_seq):
    pages = refs[:pps]
    if pps:
        o_ref, os_ref, km_ref, xb_ref, xsb_ref = refs[pps:]
    else:
        o_ref, os_ref, xb_ref, xsb_ref = refs
    i, f = pl.program_id(0), pl.program_id(1)

    if pps:
        step = i * pl.num_programs(1) + f

        @pl.when(step < job_steps)
        def _():
            jj = lax.rem(step, steps_per_seq)
            for bi in range(pps // PAGES_PER_BLOCK):
                tot = jnp.zeros((n_heads, HEAD_DIM), F32)
                for pg in range(PAGES_PER_BLOCK):
                    page = pages[PAGES_PER_BLOCK * bi + pg][...].reshape(PAGE_SIZE, n_heads, HEAD_DIM)
                    tot = tot + jnp.sum(page, axis=0)
                row0 = pl.multiple_of((jj * (pps // PAGES_PER_BLOCK) + bi) * n_heads, n_heads)
                km_ref[pl.ds(row0, n_heads), :] = tot * (1.0 / MOBA_BLOCK)

    def group_step(x_ref, xb_ref, o_ref):
        @pl.when(f == 0)
        def _():
            xb_ref[...] = x_ref[...].astype(BF16)
            o_ref[...] = jnp.zeros_like(o_ref)

        xb = xb_ref[...]
        gate = jnp.dot(xb, wg_ref[...].astype(BF16), preferred_element_type=F32)
        up = jnp.dot(xb, wu_ref[...].astype(BF16), preferred_element_type=F32)
        hid = gate * _sigmoid(gate) * up
        o_ref[...] += _dot(hid, wd_ref[...])

        @pl.when(f == pl.num_programs(1) - 1)
        def _():
            y = alpha * x_ref[...] + 0.5 * o_ref[...]
            o_ref[...] = _layer_norm(y, g_ref[...], b_ref[...])

    group_step(x_ref, xb_ref, o_ref)

    @pl.when(i == 0)
    def _():
        group_step(xs_ref, xsb_ref, os_ref)


def _ffn_ln(x, xs, wg, wu, wd, ln_g, ln_b, l, ln_idx, page_table, cache=None, n_heads=0, *, alpha, tm, tf, pps=0):
    m, d = x.shape
    ms = xs.shape[0]
    f = wg.shape[2]
    grid = (m // tm, f // tf)
    nb, n_pages = page_table.shape
    n_blk = n_pages // PAGES_PER_BLOCK
    steps_per_seq = n_pages // pps if pps else 1
    job_steps = nb * steps_per_seq if pps else 0
    assert job_steps <= grid[0] * grid[1] and (not pps or n_pages % pps == 0)

    def seq_step(i, j):
        s = jnp.minimum(i * grid[1] + j, job_steps - 1)
        return s // steps_per_seq, lax.rem(s, steps_per_seq)

    def page(k):
        return _page_spec(n_heads, lambda i, j, pt: (l, pt[seq_step(i, j)[0], pps * seq_step(i, j)[1] + k], 0, 0))

    in_specs = [
        pl.BlockSpec((tm, d), lambda i, j, pt: (i, 0), pipeline_mode=pl.Buffered(1)),
        pl.BlockSpec((ms, d), lambda i, j, pt: (0, 0), pipeline_mode=pl.Buffered(1)),
        pl.BlockSpec((None, d, tf), lambda i, j, pt: (l, 0, j)),
        pl.BlockSpec((None, d, tf), lambda i, j, pt: (l, 0, j)),
        pl.BlockSpec((None, tf, d), lambda i, j, pt: (l, j, 0)),
        pl.BlockSpec((None, 1, d), lambda i, j, pt: (ln_idx, 0, 0)),
        pl.BlockSpec((None, 1, d), lambda i, j, pt: (ln_idx, 0, 0)),
    ] + [page(k) for k in range(pps)]
    out_specs = [
        pl.BlockSpec((tm, d), lambda i, j, pt: (i, 0), pipeline_mode=pl.Buffered(1)),
        pl.BlockSpec((ms, d), lambda i, j, pt: (0, 0), pipeline_mode=pl.Buffered(1)),
    ]
    out_shape = [jax.ShapeDtypeStruct((m, d), F32), jax.ShapeDtypeStruct((ms, d), F32)]
    if pps:
        out_specs.append(pl.BlockSpec((None, n_blk * n_heads, HEAD_DIM), lambda i, j, pt: (seq_step(i, j)[0], 0, 0)))
        out_shape.append(jax.ShapeDtypeStruct((nb, n_blk * n_heads, HEAD_DIM), F32))
    return pl.pallas_call(
        functools.partial(_ffn_ln_kernel, alpha=alpha, n_heads=n_heads, pps=pps, job_steps=job_steps,
                          steps_per_seq=steps_per_seq),
        grid_spec=pltpu.PrefetchScalarGridSpec(
            num_scalar_prefetch=1, grid=grid, in_specs=in_specs, out_specs=out_specs,
            scratch_shapes=[pltpu.VMEM((tm, d), BF16), pltpu.VMEM((ms, d), BF16)]),
        out_shape=out_shape,
        compiler_params=_params(("arbitrary", "arbitrary")),
        name="ffn_ln",
    )(page_table, x, xs, wg, wu, wd, ln_g, ln_b, *([cache] * pps))


def _matmul_kernel(x_ref, w_ref, o_ref, xb_ref):
    @pl.when(pl.program_id(1) == 0)
    def _():
        xb_ref[...] = x_ref[...].astype(BF16)

    o_ref[...] = jnp.dot(xb_ref[...], w_ref[...].astype(BF16), preferred_element_type=F32)


def _matmul(x, w, l, *, tm, tn):
    m, k = x.shape
    n = w.shape[2]
    return pl.pallas_call(
        _matmul_kernel,
        grid=(m // tm, n // tn),
        in_specs=[
            pl.BlockSpec((tm, k), lambda i, j: (i, 0), pipeline_mode=pl.Buffered(1)),
            pl.BlockSpec((None, k, tn), lambda i, j: (l, 0, j)),
        ],
        out_specs=pl.BlockSpec((tm, tn), lambda i, j: (i, j)),
        out_shape=jax.ShapeDtypeStruct((m, n), F32),
        scratch_shapes=[pltpu.VMEM((tm, k), BF16)],
        compiler_params=_params(("parallel", "arbitrary")),
        name="matmul",
    )(x, w)


def _in_proj_kernel(*refs, k_lo, v_lo, n_kv, aliased):
    x_ref, w_ref = refs[:2]
    z_ref, ko_ref, vo_ref, xb_ref = refs[4:] if aliased else refs[2:]
    j = pl.program_id(1)

    @pl.when(j == 0)
    def _():
        xb_ref[...] = x_ref[...].astype(BF16)

    res = jnp.dot(xb_ref[...], w_ref[...].astype(BF16), preferred_element_type=F32)
    z_ref[...] = res

    @pl.when((j >= k_lo) & (j < k_lo + n_kv))
    def _():
        ko_ref[...] = res

    @pl.when((j >= v_lo) & (j < v_lo + n_kv))
    def _():
        vo_ref[...] = res


def _in_proj(x, w, l, k_col, v_col, kv_width, stacks, *, tm, tn):
    m, k = x.shape
    nl, _, n = w.shape
    assert k_col % tn == 0 and v_col % tn == 0 and kv_width % tn == 0
    k_lo, v_lo, n_kv = k_col // tn, v_col // tn, kv_width // tn
    aliased = stacks is not None

    def kv_spec(lo):
        return pl.BlockSpec((None, tm, tn), lambda i, j: (l, i, jnp.clip(j - lo, 0, n_kv - 1)),
                            pipeline_mode=pl.Buffered(1))

    stack_shape = jax.ShapeDtypeStruct((nl, m, kv_width), F32)
    return pl.pallas_call(
        functools.partial(_in_proj_kernel, k_lo=k_lo, v_lo=v_lo, n_kv=n_kv, aliased=aliased),
        grid=(m // tm, n // tn),
        in_specs=[
            pl.BlockSpec((tm, k), lambda i, j: (i, 0), pipeline_mode=pl.Buffered(1)),
            pl.BlockSpec((None, k, tn), lambda i, j: (l, 0, j)),
        ] + ([pl.BlockSpec(memory_space=pl.ANY)] * 2 if aliased else []),
        out_specs=[pl.BlockSpec((tm, tn), lambda i, j: (i, j)), kv_spec(k_lo), kv_spec(v_lo)],
        out_shape=[jax.ShapeDtypeStruct((m, n), F32), stack_shape, stack_shape],
        input_output_aliases={2: 1, 3: 2} if aliased else {},
        scratch_shapes=[pltpu.VMEM((tm, k), BF16)],
        compiler_params=_params(("arbitrary", "arbitrary")),
        name="in_proj",
    )(x, w, *(stacks if aliased else ()))


def _pool_kernel(u_ref, buf_ref, w_ref, scale_ref, o_ref, xx_ref, *, tl, start):
    ti = pl.program_id(1)

    @pl.when(ti == 0)
    def _():
        xx_ref[0:POOL_HALO, :] = buf_ref[...]

    u = u_ref[...]
    xx_ref[POOL_HALO:POOL_HALO + tl, :] = u
    pos = start + ti * tl + lax.broadcasted_iota(jnp.int32, (tl, 1), 0)
    for gi, w in enumerate(POOL_WINDOWS):
        cols = slice(gi * LANE, (gi + 1) * LANE)
        win = xx_ref[POOL_HALO:POOL_HALO + tl, cols]
        for i in range(1, w):
            win = win + xx_ref[POOL_HALO - i:POOL_HALO - i + tl, cols]
        cnt = jnp.minimum(pos + 1, w).astype(F32)
        pooled = win / cnt - u[:, cols]
        o_ref[:, cols] = (_dot3(pooled, w_ref[gi]) * scale_ref[:, cols]).astype(o_ref.dtype)
    xx_ref[0:POOL_HALO, :] = xx_ref[tl:tl + POOL_HALO, :]


def _pool(z, buf16, pool_w, pool_scale, l, *, tl, start, out_dtype):
    b, t, _ = z.shape
    w = buf16.shape[2]
    assert w == len(POOL_WINDOWS) * LANE
    return pl.pallas_call(
        functools.partial(_pool_kernel, tl=tl, start=start),
        grid=(b, t // tl),
        in_specs=[
            pl.BlockSpec((None, tl, w), lambda bi, ti: (bi, ti, 0)),
            pl.BlockSpec((None, POOL_HALO, w), lambda bi, ti: (bi, 0, 0)),
            pl.BlockSpec((None, len(POOL_WINDOWS), LANE, LANE), lambda bi, ti: (l, 0, 0, 0)),
            pl.BlockSpec((None, 1, w), lambda bi, ti: (l, 0, 0)),
        ],
        out_specs=pl.BlockSpec((None, tl, w), lambda bi, ti: (bi, ti, 0)),
        out_shape=jax.ShapeDtypeStruct((b, t, w), out_dtype),
        scratch_shapes=[pltpu.VMEM((POOL_HALO + tl, w), F32)],
        compiler_params=_params(("parallel", "arbitrary")),
        name="pool",
    )(z, buf16, pool_w, pool_scale)


def _s5_kernel(u_ref, s0r_ref, s0i_ref, lr_ref, li_ref, ldt_ref, br_ref, bi_ref, cr_ref, ci_ref,
               d_ref, wg_ref, bg_ref, y_ref, sr_ref, si_ref,
               pr_ref, pi_ref, qr_ref, qi_ref, apr_ref, api_ref, zr_ref, zi_ref, car_ref, cai_ref,
               *, tl, pad, n_chunk):
    ti = pl.program_id(1)
    n_state = lr_ref.shape[1]
    n_pass = int(math.log2(tl))
    uc = u_ref.shape[1] // n_chunk
    sc = n_state // n_chunk

    @pl.when(ti == 0)
    def _():
        dt = jnp.exp(ldt_ref[...])
        lr, li = lr_ref[...], li_ref[...]
        mag = jnp.exp(lr * dt)
        a_re, a_im = mag * jnp.cos(li * dt), mag * jnp.sin(li * dt)
        den = lr * lr + li * li
        zr_ref[...] = ((a_re - 1.0) * lr + a_im * li) / den
        zi_ref[...] = (a_im * lr - (a_re - 1.0) * li) / den
        pw_re, pw_im = a_re, a_im
        for j in range(n_pass):
            apr_ref[j:j + 1, :] = pw_re
            api_ref[j:j + 1, :] = pw_im
            pw_re, pw_im = pw_re * pw_re - pw_im * pw_im, 2.0 * (pw_re * pw_im)
        car_ref[...] = s0r_ref[...]
        cai_ref[...] = s0i_ref[...]
        zeros = jnp.zeros((pad, n_state), F32)
        pr_ref[0:pad, :] = zeros
        pi_ref[0:pad, :] = zeros
        qr_ref[0:pad, :] = zeros
        qi_ref[0:pad, :] = zeros

    u = u_ref[...]
    for c in range(n_chunk):
        uu = u[:, c * uc:(c + 1) * uc]
        ub_re = _dot(uu, br_ref[c])
        ub_im = _dot(uu, bi_ref[c])
        cols = slice(c * sc, (c + 1) * sc)
        z_re, z_im = zr_ref[:, cols], zi_ref[:, cols]
        pr_ref[pad:pad + tl, cols] = z_re * ub_re - z_im * ub_im
        pi_ref[pad:pad + tl, cols] = z_re * ub_im + z_im * ub_re
    a_re, a_im = apr_ref[0:1, :], api_ref[0:1, :]
    c_re, c_im = car_ref[...], cai_ref[...]
    pr_ref[pad:pad + 1, :] += a_re * c_re - a_im * c_im
    pi_ref[pad:pad + 1, :] += a_re * c_im + a_im * c_re

    bufs = ((pr_ref, pi_ref), (qr_ref, qi_ref))
    for j in range(n_pass):
        k = 1 << j
        (s_re, s_im), (d_re, d_im) = bufs[j % 2], bufs[(j + 1) % 2]

        def col_body(cb, carry, k=k, j=j, s_re=s_re, s_im=s_im, d_re=d_re, d_im=d_im):
            for sub in range(SCAN_COLS):
                cols = pl.ds(pl.multiple_of((cb * SCAN_COLS + sub) * LANE, LANE), LANE)
                ar, ai = apr_ref[j:j + 1, cols], api_ref[j:j + 1, cols]
                cur_re, cur_im = s_re[pad:pad + tl, cols], s_im[pad:pad + tl, cols]
                sh_re, sh_im = s_re[pad - k:pad - k + tl, cols], s_im[pad - k:pad - k + tl, cols]
                d_re[pad:pad + tl, cols] = cur_re + (ar * sh_re - ai * sh_im)
                d_im[pad:pad + tl, cols] = cur_im + (ar * sh_im + ai * sh_re)
            return carry

        lax.fori_loop(0, n_state // (SCAN_COLS * LANE), col_body, 0)
    f_re, f_im = bufs[n_pass % 2]

    car_ref[...] = f_re[pad + tl - 1:pad + tl, :]
    cai_ref[...] = f_im[pad + tl - 1:pad + tl, :]
    sr_ref[...] = car_ref[...]
    si_ref[...] = cai_ref[...]

    ys = []
    for c in range(n_chunk):
        cols = slice(c * sc, (c + 1) * sc)
        ys.append(_dot(f_re[pad:pad + tl, cols], cr_ref[c]) - _dot(f_im[pad:pad + tl, cols], ci_ref[c]))
    y = jnp.concatenate(ys, axis=1) + d_ref[...] * u
    y = 0.5 * y * (1.0 + jnp.tanh(math.sqrt(2.0 / math.pi) * (y + 0.044715 * (y * y * y))))
    y_ref[...] = (y * _sigmoid(_dot(y, wg_ref[...]) + bg_ref[...])).astype(y_ref.dtype)


def _s5_seg_kernel(u_ref, s0r_ref, s0i_ref, lr_ref, li_ref, ldt_ref, br_ref, bi_ref, cr_ref, ci_ref,
                   d_ref, wg_ref, bg_ref, y_ref, sr_ref, si_ref,
                   pr_ref, pi_ref, qr_ref, qi_ref, apr_ref, api_ref, zr_ref, zi_ref, car_ref, cai_ref,
                   axr_ref, axi_ref, e0r_ref, e0i_ref, e1r_ref, e1i_ref, pm_ref, pmt_ref,
                   *, tl, pad, n_chunk):
    ti = pl.program_id(1)
    n_state = lr_ref.shape[1]
    seg_len = tl // SUBLANES
    n_local = int(math.log2(seg_len))
    n_link = int(math.log2(SUBLANES))
    uc = u_ref.shape[1] // n_chunk
    sc = n_state // n_chunk

    def cmul(ar, ai, br, bi):
        return ar * br - ai * bi, ar * bi + ai * br

    @pl.when(ti == 0)
    def _():
        dt = jnp.exp(ldt_ref[...])
        lr, li = lr_ref[...], li_ref[...]
        mag = jnp.exp(lr * dt)
        a_re, a_im = mag * jnp.cos(li * dt), mag * jnp.sin(li * dt)
        den = lr * lr + li * li
        zr_ref[...] = ((a_re - 1.0) * lr + a_im * li) / den
        zi_ref[...] = (a_im * lr - (a_re - 1.0) * li) / den
        pw_re, pw_im = a_re, a_im
        for j in range(n_local + n_link):
            apr_ref[j:j + 1, :] = pw_re
            api_ref[j:j + 1, :] = pw_im
            pw_re, pw_im = pw_re * pw_re - pw_im * pw_im, 2.0 * (pw_re * pw_im)
        axr_ref[0:SUBLANES, :] = jnp.broadcast_to(a_re, (SUBLANES, n_state))
        axi_ref[0:SUBLANES, :] = jnp.broadcast_to(a_im, (SUBLANES, n_state))
        for j in range(n_local):
            n_rows = SUBLANES << j
            nr, ni = cmul(axr_ref[0:n_rows, :], axi_ref[0:n_rows, :], apr_ref[j:j + 1, :], api_ref[j:j + 1, :])
            axr_ref[n_rows:2 * n_rows, :] = nr
            axi_ref[n_rows:2 * n_rows, :] = ni
        car_ref[...] = s0r_ref[...]
        cai_ref[...] = s0i_ref[...]
        zeros = jnp.zeros((pad, n_state), F32)
        for ref in (pr_ref, pi_ref, qr_ref, qi_ref):
            ref[0:pad, :] = zeros
        for ref in (e0r_ref, e0i_ref, e1r_ref, e1i_ref):
            ref[0:SUBLANES, :] = jnp.zeros((SUBLANES, n_state), F32)
        i0 = lax.broadcasted_iota(jnp.int32, (tl, tl), 0)
        i1 = lax.broadcasted_iota(jnp.int32, (tl, tl), 1)
        dest = lambda t: jnp.left_shift(jnp.bitwise_and(t, seg_len - 1), n_link) + jnp.right_shift(t, n_local)
        pm_ref[...] = jnp.where(i1 == dest(i0), 1.0, 0.0).astype(BF16)
        pmt_ref[...] = jnp.where(i0 == dest(i1), 1.0, 0.0).astype(BF16)

    u = u_ref[...]
    up = jnp.dot(pmt_ref[...], u.astype(BF16), preferred_element_type=F32).astype(BF16)
    for c in range(n_chunk):
        uu = up[:, c * uc:(c + 1) * uc]
        ub_re = _dot(uu, br_ref[c])
        ub_im = _dot(uu, bi_ref[c])
        cols = slice(c * sc, (c + 1) * sc)
        z_re, z_im = zr_ref[:, cols], zi_ref[:, cols]
        pr_ref[pad:pad + tl, cols] = z_re * ub_re - z_im * ub_im
        pi_ref[pad:pad + tl, cols] = z_re * ub_im + z_im * ub_re
    add_re, add_im = cmul(apr_ref[0:1, :], api_ref[0:1, :], car_ref[...], cai_ref[...])
    pr_ref[pad:pad + 1, :] += add_re
    pi_ref[pad:pad + 1, :] += add_im

    bufs = ((pr_ref, pi_ref), (qr_ref, qi_ref))
    for j in range(n_local):
        k = SUBLANES << j
        (s_re, s_im), (d_re, d_im) = bufs[j % 2], bufs[(j + 1) % 2]

        def col_body(cb, carry, k=k, j=j, s_re=s_re, s_im=s_im, d_re=d_re, d_im=d_im):
            for sub in range(SCAN_COLS):
                cols = pl.ds(pl.multiple_of((cb * SCAN_COLS + sub) * LANE, LANE), LANE)
                mr, mi = cmul(apr_ref[j:j + 1, cols], api_ref[j:j + 1, cols],
                              s_re[pad - k:pad - k + tl, cols], s_im[pad - k:pad - k + tl, cols])
                d_re[pad:pad + tl, cols] = s_re[pad:pad + tl, cols] + mr
                d_im[pad:pad + tl, cols] = s_im[pad:pad + tl, cols] + mi
            return carry

        lax.fori_loop(0, n_state // (SCAN_COLS * LANE), col_body, 0)
    f_re, f_im = bufs[n_local % 2]

    e_bufs = ((e0r_ref, e0i_ref), (e1r_ref, e1i_ref))
    lo, hi = SUBLANES, 2 * SUBLANES
    e0r_ref[lo:hi, :] = f_re[pad + tl - SUBLANES:pad + tl, :]
    e0i_ref[lo:hi, :] = f_im[pad + tl - SUBLANES:pad + tl, :]
    for jj in range(n_link):
        k = 1 << jj
        (s_re, s_im), (d_re, d_im) = e_bufs[jj % 2], e_bufs[(jj + 1) % 2]
        mr, mi = cmul(apr_ref[n_local + jj:n_local + jj + 1, :], api_ref[n_local + jj:n_local + jj + 1, :],
                      s_re[lo - k:hi - k, :], s_im[lo - k:hi - k, :])
        d_re[lo:hi, :] = s_re[lo:hi, :] + mr
        d_im[lo:hi, :] = s_im[lo:hi, :] + mi
    er_ref, ei_ref = e_bufs[n_link % 2]
    car_ref[...] = er_ref[hi - 1:hi, :]
    cai_ref[...] = ei_ref[hi - 1:hi, :]
    sr_ref[...] = car_ref[...]
    si_ref[...] = cai_ref[...]

    def apply_body(cb, carry):
        for sub in range(SCAN_COLS):
            cols = pl.ds(pl.multiple_of((cb * SCAN_COLS + sub) * LANE, LANE), LANE)
            pe_re, pe_im = er_ref[lo - 1:hi - 1, cols], ei_ref[lo - 1:hi - 1, cols]
            for i in range(seg_len):
                rows = slice(i * SUBLANES, (i + 1) * SUBLANES)
                dst = slice(pad + i * SUBLANES, pad + (i + 1) * SUBLANES)
                mr, mi = cmul(axr_ref[rows, cols], axi_ref[rows, cols], pe_re, pe_im)
                f_re[dst, cols] += mr
                f_im[dst, cols] += mi
        return carry

    lax.fori_loop(0, n_state // (SCAN_COLS * LANE), apply_body, 0)

    ys = []
    for c in range(n_chunk):
        cols = slice(c * sc, (c + 1) * sc)
        ys.append(_dot(f_re[pad:pad + tl, cols], cr_ref[c]) - _dot(f_im[pad:pad + tl, cols], ci_ref[c]))
    y_hi, y_lo = _split(jnp.concatenate(ys, axis=1))
    pm = pm_ref[...]
    y = (jnp.dot(pm, y_hi, preferred_element_type=F32) + jnp.dot(pm, y_lo, preferred_element_type=F32)
         + d_ref[...] * u)
    y = 0.5 * y * (1.0 + jnp.tanh(math.sqrt(2.0 / math.pi) * (y + 0.044715 * (y * y * y))))
    y_ref[...] = (y * _sigmoid(_dot(y, wg_ref[...]) + bg_ref[...])).astype(y_ref.dtype)


def _s5(z, col0, s0_re, s0_im, lam_re, lam_im, ldt_rep, b_re_c, b_im_c, c_re_c, c_im_c, ssm_d, glu_w, glu_b, l,
        *, tl, out_dtype):
    b, t, _ = z.shape
    n_state = s0_re.shape[2]
    n_chunk, uc, sc = b_re_c.shape[1:]
    u_w = n_chunk * uc
    assert col0 % u_w == 0 and n_chunk * sc == n_state
    pad = max(SUBLANES, tl // 2)
    n_pass = int(math.log2(tl))
    assert 1 << n_pass == tl
    vec = lambda w: pl.BlockSpec((None, 1, w), lambda bi, ti: (l, 0, 0))
    state_spec = pl.BlockSpec((None, 1, n_state), lambda bi, ti: (bi, 0, 0))
    scratch = ([pltpu.VMEM((pad + tl, n_state), F32)] * 4
               + [pltpu.VMEM((max(SUBLANES, n_pass), n_state), F32)] * 2
               + [pltpu.VMEM((1, n_state), F32)] * 4)
    segmented = tl >= SUBLANES * SUBLANES
    if segmented:
        scratch += ([pltpu.VMEM((tl, n_state), F32)] * 2 + [pltpu.VMEM((2 * SUBLANES, n_state), F32)] * 4
                    + [pltpu.VMEM((tl, tl), BF16)] * 2)
    return pl.pallas_call(
        functools.partial(_s5_seg_kernel if segmented else _s5_kernel, tl=tl, pad=pad, n_chunk=n_chunk),
        grid=(b, t // tl),
        in_specs=[
            pl.BlockSpec((None, tl, u_w), lambda bi, ti: (bi, ti, col0 // u_w)),
            state_spec, state_spec,
            vec(n_state), vec(n_state), vec(n_state),
            pl.BlockSpec((None, n_chunk, uc, sc), lambda bi, ti: (l, 0, 0, 0)),
            pl.BlockSpec((None, n_chunk, uc, sc), lambda bi, ti: (l, 0, 0, 0)),
            pl.BlockSpec((None, n_chunk, sc, uc), lambda bi, ti: (l, 0, 0, 0)),
            pl.BlockSpec((None, n_chunk, sc, uc), lambda bi, ti: (l, 0, 0, 0)),
            vec(u_w),
            pl.BlockSpec((None, u_w, u_w), lambda bi, ti: (l, 0, 0)),
            vec(u_w),
        ],
        out_specs=[
            pl.BlockSpec((None, tl, u_w), lambda bi, ti: (bi, ti, 0)),
            state_spec, state_spec,
        ],
        out_shape=[
            jax.ShapeDtypeStruct((b, t, u_w), out_dtype),
            jax.ShapeDtypeStruct((b, 1, n_state), F32),
            jax.ShapeDtypeStruct((b, 1, n_state), F32),
        ],
        scratch_shapes=scratch,
        compiler_params=_params(("parallel", "arbitrary")),
        name="s5",
    )(z, s0_re, s0_im, lam_re, lam_im, ldt_rep, b_re_c, b_im_c, c_re_c, c_im_c, ssm_d, glu_w, glu_b)


def _block_diag_chunks(w, n_chunk):
    nl, g, r, c = w.shape
    gc = g // n_chunk
    w = w.reshape(nl, n_chunk, gc, r, c)
    eye = jnp.eye(gc, dtype=w.dtype)
    out = w[:, :, :, :, None, :] * eye[None, None, :, None, :, None]
    return out.reshape(nl, n_chunk, gc * r, gc * c)


def _top_blocks(gate, n_valid_col):
    lane = lax.broadcasted_iota(jnp.int32, gate.shape, 1)
    lane_f = lane.astype(F32)
    gate = jnp.where(lane < n_valid_col, gate, NEG_INF)
    sel = jnp.zeros(gate.shape, F32)
    for _ in range(MOBA_TOPK):
        m = jnp.max(gate, axis=1, keepdims=True)
        idx = jnp.min(jnp.where(gate == m, lane_f, float(gate.shape[1])), axis=1, keepdims=True)
        hit = lane_f == idx
        valid = jnp.broadcast_to(m, gate.shape) > 0.5 * NEG_INF
        sel = jnp.where(hit & valid, 1.0, sel)
        gate = jnp.where(hit, NEG_INF, gate)
    return sel


def _pick_column(sel_b, j):
    k = lax.broadcasted_iota(jnp.int32, (LANE, LANE), 0)
    onehot = jnp.where(k == j, 1.0, 0.0).astype(BF16)
    return jnp.dot(sel_b, onehot, preferred_element_type=F32) > 0.5


def _lane_tile(x, k):
    return x if k == 1 else jnp.concatenate([x] * k, axis=1)


def _moba_prompt_kernel(slope_ref, q_ref, k_ref, v_ref, o_ref, kb_ref, vb_ref, km_ref, *, n_blk, hp, cb, scale):
    hg = pl.program_id(1)
    n = pl.program_id(2)
    blk = MOBA_BLOCK
    wc = cb * blk
    rows = hp * blk
    t = k_ref.shape[0]
    stack = lambda per_head: jnp.concatenate(per_head, axis=0)
    head_rows = lambda x, hh: x[hh * blk:(hh + 1) * blk]

    @pl.when(n == 0)
    def _():
        km_ref[...] = jnp.zeros_like(km_ref)
        for hh in range(hp):
            cols = slice(hh * HEAD_DIM, (hh + 1) * HEAD_DIM)
            kb_ref[hh] = k_ref[:, cols].astype(BF16)
            vb_ref[hh, :, 0:HEAD_DIM] = v_ref[:, cols].astype(BF16)
            vb_ref[hh, :, HEAD_DIM:] = jnp.ones((t, HEAD_DIM), BF16)
            for j in range(n_blk):
                km_ref[hh, j:j + 1, :] = jnp.mean(k_ref[j * blk:(j + 1) * blk, cols], axis=0, keepdims=True)

    slopes = [slope_ref[hg * hp + hh] for hh in range(hp)]
    cpos = lax.broadcasted_iota(jnp.int32, (1, wc), 1).astype(F32)
    col_bias = [s * cpos for s in slopes]
    slope_rows = stack([jnp.zeros((blk, LANE), F32) + s for s in slopes])
    qs = [q_ref[:, hh * HEAD_DIM:(hh + 1) * HEAD_DIM] for hh in range(hp)]
    qbs = [(q * scale).astype(BF16) for q in qs]
    gate = stack([_dot_nt(q, km_ref[hh]) for hh, q in enumerate(qs)])
    sel_b = _top_blocks(gate, n).astype(BF16)
    rowmax = lambda x: jnp.broadcast_to(jnp.max(x, axis=1, keepdims=True), (rows, LANE))

    own0 = pl.multiple_of(n * blk, blk)
    causal = stack([lax.broadcasted_iota(jnp.int32, (blk, blk), 0)
                    >= lax.broadcasted_iota(jnp.int32, (blk, blk), 1)] * hp)
    s = stack([_dot_nt(qb, kb_ref[hh, pl.ds(own0, blk), :]) + col_bias[hh][:, :blk] for hh, qb in enumerate(qbs)])
    s = jnp.where(causal, s, NEG_INF)
    m0 = rowmax(s)
    p = jnp.exp(s - _lane_tile(m0, blk // LANE))
    acc0 = stack([_dot(head_rows(p, hh), vb_ref[hh, pl.ds(own0, blk), :]) for hh in range(hp)])

    def body(c, carry):
        m, acc = carry
        c0 = pl.multiple_of(c * wc, wc)
        shift = slope_rows * (n * blk - c * wc).astype(F32)
        s = stack([_dot_nt(qb, kb_ref[hh, pl.ds(c0, wc), :]) + col_bias[hh] for hh, qb in enumerate(qbs)])
        groups = [s[:, g * blk:(g + 1) * blk] for g in range(cb)]
        chosen = [_pick_column(sel_b, c * cb + g) for g in range(cb)]
        m_new = m
        for g in range(cb):
            m_new = jnp.maximum(m_new, jnp.where(chosen[g], rowmax(groups[g]) - shift, NEG_INF))
        p = jnp.concatenate(
            [jnp.exp(groups[g] - _lane_tile(jnp.where(chosen[g], m_new + shift, -NEG_INF), blk // LANE))
             for g in range(cb)], axis=1)
        pv = stack([_dot(head_rows(p, hh), vb_ref[hh, pl.ds(c0, wc), :]) for hh in range(hp)])
        return m_new, _lane_tile(jnp.exp(m - m_new), 2 * HEAD_DIM // LANE) * acc + pv

    _, acc = lax.fori_loop(0, (n + cb - 1) // cb, body, (m0, acc0))
    for hh in range(hp):
        a = head_rows(acc, hh)
        o_ref[:, hh * HEAD_DIM:(hh + 1) * HEAD_DIM] = (a[:, :HEAD_DIM] / a[:, HEAD_DIM:]).astype(o_ref.dtype)


def _moba_prompt(z, slopes, q_col, k_col, v_col, n_heads, *, hp, cb):
    b, t, _ = z.shape
    n_blk = t // MOBA_BLOCK
    w = hp * HEAD_DIM
    assert n_blk * MOBA_BLOCK == t and n_blk <= LANE and n_heads % hp == 0 and n_blk % cb == 0
    assert q_col % w == 0 and k_col % w == 0 and v_col % w == 0
    qc, kc, vc = q_col // w, k_col // w, v_col // w
    whole = lambda c: pl.BlockSpec((None, t, w), lambda bi, hg, n, s: (bi, 0, c + hg),
                                   pipeline_mode=pl.Buffered(1))
    return pl.pallas_call(
        functools.partial(_moba_prompt_kernel, n_blk=n_blk, hp=hp, cb=cb, scale=HEAD_DIM ** -0.5),
        grid_spec=pltpu.PrefetchScalarGridSpec(
            num_scalar_prefetch=1,
            grid=(b, n_heads // hp, n_blk),
            in_specs=[
                pl.BlockSpec((None, MOBA_BLOCK, w), lambda bi, hg, n, s: (bi, n, qc + hg)),
                whole(kc), whole(vc),
            ],
            out_specs=pl.BlockSpec((None, MOBA_BLOCK, w), lambda bi, hg, n, s: (bi, n, hg)),
            scratch_shapes=[pltpu.VMEM((hp, t, HEAD_DIM), BF16), pltpu.VMEM((hp, t, 2 * HEAD_DIM), BF16),
                            pltpu.VMEM((hp, LANE, HEAD_DIM), F32)],
        ),
        out_shape=jax.ShapeDtypeStruct((b, t, n_heads * HEAD_DIM), BF16),
        compiler_params=_params(("parallel", "parallel", "arbitrary")),
        name="moba_prompt",
    )(slopes, z, z, z)


def _page_spec(n_heads, index_map):
    return pl.BlockSpec((None, None, PAGE_SIZE * n_heads, HEAD_DIM), index_map)


def _head_major(page_ref, n_heads):
    return jnp.concatenate([page_ref[pl.ds(h, PAGE_SIZE, stride=n_heads), :] for h in range(n_heads)], axis=1)


def _moba_sample_kernel(pt_ref, slope_ref, q_ref, kn_ref, vn_ref, km_ref, *refs,
                        n_heads, n_blk, bps, past_len, scale):
    npg = PAGES_PER_BLOCK * bps
    k_refs, v_refs = refs[:npg], refs[npg:2 * npg]
    o_ref, qbd_ref, sel_ref, slp_ref, m_ref, l_ref, acc_ref = refs[2 * npg:]
    jj = pl.program_id(1)
    t, width = q_ref.shape
    rows = n_heads * t
    rid = lax.broadcasted_iota(jnp.int32, (rows, 1), 0).astype(F32)
    hq = jnp.floor((rid + 0.5) * (1.0 / t))
    tq = rid - t * hq

    @pl.when(jj == 0)
    def _():
        q = q_ref[...]
        c = lax.broadcasted_iota(jnp.int32, (t, width), 1)
        per_head = [jnp.where((c >= h * HEAD_DIM) & (c < (h + 1) * HEAD_DIM), q, 0.0) for h in range(n_heads)]
        qbd_ref[...] = (jnp.concatenate(per_head, axis=0) * scale).astype(BF16)
        gates = []
        for h in range(n_heads):
            km_h = km_ref[pl.ds(h, n_blk, stride=n_heads), :]
            km_h = jnp.concatenate([km_h, jnp.zeros((LANE - n_blk, HEAD_DIM), F32)], axis=0)
            gates.append(_dot_nt(q[:, h * HEAD_DIM:(h + 1) * HEAD_DIM], km_h))
        sel_ref[...] = _top_blocks(jnp.concatenate(gates, axis=0), n_blk).astype(BF16)
        slp = jnp.zeros((rows, 1), F32)
        for h in range(n_heads):
            slp = jnp.where(hq == h, slope_ref[h], slp)
        slp_ref[...] = slp
        fill = jnp.zeros((LANE - t, width), F32)
        s = _dot_nt(qbd_ref[...], jnp.concatenate([kn_ref[...], fill], axis=0))
        cpos = lax.broadcasted_iota(jnp.int32, (rows, LANE), 1).astype(F32)
        s = jnp.where(cpos <= tq, s - slp * (tq - cpos), NEG_INF)
        m0 = jnp.max(s, axis=1, keepdims=True)
        p = jnp.exp(s - m0)
        m_ref[...] = m0
        l_ref[...] = jnp.sum(p, axis=1, keepdims=True)
        acc_ref[...] = _dot(p, jnp.concatenate([vn_ref[...], fill], axis=0))

    slp = slp_ref[...]
    qbd = qbd_ref[...]
    sel_b = sel_ref[...]
    kmat = jnp.concatenate([_head_major(r, n_heads) for r in k_refs], axis=0).astype(BF16)
    vmat = jnp.concatenate([_head_major(r, n_heads) for r in v_refs], axis=0).astype(BF16)
    kpos = lax.broadcasted_iota(jnp.int32, (rows, bps * MOBA_BLOCK), 1).astype(F32)
    s = _dot_nt(qbd, kmat) - slp * (tq - kpos)
    shift = slp * (past_len - jj * (bps * MOBA_BLOCK)).astype(F32)
    groups = [s[:, g * MOBA_BLOCK:(g + 1) * MOBA_BLOCK] for g in range(bps)]
    chosen = [_pick_column(sel_b, jj * bps + g)[:, 0:1] for g in range(bps)]
    m = m_ref[...]
    m_new = m
    for g in range(bps):
        m_new = jnp.maximum(m_new, jnp.where(chosen[g], jnp.max(groups[g], axis=1, keepdims=True) - shift, NEG_INF))
    p = jnp.concatenate([jnp.exp(groups[g] - jnp.where(chosen[g], m_new + shift, -NEG_INF)) for g in range(bps)],
                        axis=1)
    corr = jnp.exp(m - m_new)
    lsum = corr * l_ref[...] + jnp.sum(p, axis=1, keepdims=True)
    acc = corr * acc_ref[...] + _dot(p, vmat)
    m_ref[...], l_ref[...], acc_ref[...] = m_new, lsum, acc

    @pl.when(jj == pl.num_programs(1) - 1)
    def _():
        out = acc / lsum
        o_ref[...] = jnp.concatenate(
            [out[h * t:(h + 1) * t, h * HEAD_DIM:(h + 1) * HEAD_DIM] for h in range(n_heads)], axis=1)


def _moba_sample(z, kmeans, slopes, cache_k, cache_v, page_table, l, q_col, k_col, v_col, n_heads, *, bps):
    b, t, _ = z.shape
    n_pages = page_table.shape[1]
    n_blk = n_pages // PAGES_PER_BLOCK
    width = n_heads * HEAD_DIM
    npg = PAGES_PER_BLOCK * bps
    rows = n_heads * t
    assert n_blk % bps == 0 and n_blk <= LANE and t <= LANE and rows % 16 == 0
    assert q_col % width == 0 and k_col % width == 0 and v_col % width == 0

    def page(k):
        return _page_spec(n_heads, lambda bi, jj, pt, sl: (l, pt[bi, npg * jj + k], 0, 0))

    tok = lambda col: pl.BlockSpec((None, t, width), lambda bi, jj, pt, sl: (bi, 0, col // width))
    return pl.pallas_call(
        functools.partial(_moba_sample_kernel, n_heads=n_heads, n_blk=n_blk, bps=bps,
                          past_len=n_pages * PAGE_SIZE, scale=HEAD_DIM ** -0.5),
        grid_spec=pltpu.PrefetchScalarGridSpec(
            num_scalar_prefetch=2,
            grid=(b, n_blk // bps),
            in_specs=[tok(q_col), tok(k_col), tok(v_col),
                      pl.BlockSpec((None, n_blk * n_heads, HEAD_DIM), lambda bi, jj, pt, sl: (bi, 0, 0))]
            + [page(k) for k in range(npg)] + [page(k) for k in range(npg)],
            out_specs=pl.BlockSpec((None, t, width), lambda bi, jj, pt, sl: (bi, 0, 0)),
            scratch_shapes=[pltpu.VMEM((rows, width), BF16), pltpu.VMEM((rows, LANE), BF16),
                            pltpu.VMEM((rows, 1), F32), pltpu.VMEM((rows, 1), F32), pltpu.VMEM((rows, 1), F32),
                            pltpu.VMEM((rows, width), F32)],
        ),
        out_shape=jax.ShapeDtypeStruct((b, t, width), F32),
        compiler_params=_params(("parallel", "arbitrary")),
        name="moba_sample",
    )(page_table, slopes, z, z, z, kmeans, *([cache_k] * npg), *([cache_v] * npg))


def _merge_ln_kernel(a_ref, s_ref, c_ref, g0_ref, g1_ref, g2_ref, bg0_ref, bg1_ref, bg2_ref,
                     wp_ref, ws_ref, wa_ref, wo_ref, h_ref, g_ref, b_ref, o_ref, *, alpha):
    j = pl.program_id(1)

    @pl.when(j == 0)
    def _():
        o_ref[...] = jnp.zeros_like(o_ref)

    gated = lambda g_ref, bg_ref, x_ref, w_ref: (
        (jnp.tanh(0.5 * (g_ref[...] + bg_ref[...])) + 1.0) * _dot(x_ref[...], w_ref[...]))
    merged = 0.5 * (gated(g0_ref, bg0_ref, a_ref, wp_ref) + gated(g1_ref, bg1_ref, s_ref, ws_ref)
                    + gated(g2_ref, bg2_ref, c_ref, wa_ref))
    o_ref[...] += _dot(merged, wo_ref[...])

    @pl.when(j == pl.num_programs(1) - 1)
    def _():
        o_ref[...] = _layer_norm(alpha * h_ref[...] + o_ref[...], g_ref[...], b_ref[...])


def _merge_ln(a_out, s_out, c_out, z, gate_col, b_gate, wp, ws, wa, wo, h, ln_g, ln_b, l, ln_idx,
              *, alpha, tm, tn):
    m, d = h.shape
    gc = gate_col // tn
    nj = d // tn
    row = lambda w: pl.BlockSpec((tm, w), lambda i, j: (i, 0), pipeline_mode=pl.Buffered(1))
    gate = lambda g: pl.BlockSpec((tm, tn), lambda i, j: (i, gc + g * nj + j))
    bias = lambda g: pl.BlockSpec((None, 1, tn), lambda i, j: (l, 0, g * nj + j))
    wcol = lambda k: pl.BlockSpec((None, k, tn), lambda i, j: (l, 0, j))
    return pl.pallas_call(
        functools.partial(_merge_ln_kernel, alpha=alpha),
        grid=(m // tm, nj),
        in_specs=[
            row(a_out.shape[1]), row(s_out.shape[1]), row(c_out.shape[1]),
            gate(0), gate(1), gate(2), bias(0), bias(1), bias(2),
            wcol(a_out.shape[1]), wcol(s_out.shape[1]), wcol(c_out.shape[1]),
            pl.BlockSpec((None, tn, d), lambda i, j: (l, j, 0)),
            row(d),
            pl.BlockSpec((None, 1, d), lambda i, j: (ln_idx, 0, 0)),
            pl.BlockSpec((None, 1, d), lambda i, j: (ln_idx, 0, 0)),
        ],
        out_specs=pl.BlockSpec((tm, d), lambda i, j: (i, 0), pipeline_mode=pl.Buffered(1)),
        out_shape=jax.ShapeDtypeStruct((m, d), F32),
        compiler_params=_params(("parallel", "arbitrary")),
        name="merge_ln",
    )(a_out, s_out, c_out, z, z, z, b_gate, b_gate, b_gate, wp, ws, wa, wo, h, ln_g, ln_b)


def _memattn_ln_kernel(h_ref, wq_ref, mk_ref, mv_ref, wo_ref, g_ref, b_ref, o_ref, *, alpha, scale):
    hx = h_ref[...]
    q = _dot(hx, wq_ref[...])
    mk, mv = mk_ref[...], mv_ref[...]
    outs = []
    for hd in range(N_MEM_HEADS):
        cols = slice(hd * MEM_HEAD_DIM, (hd + 1) * MEM_HEAD_DIM)
        s = _dot_nt(q[:, cols], mk[:, cols]) * scale
        s = s - jnp.max(s, axis=1, keepdims=True)
        p = jnp.exp(s)
        p = p / jnp.sum(p, axis=1, keepdims=True)
        outs.append(_dot(p, mv[:, cols]))
    o = jnp.concatenate(outs, axis=1)
    o_ref[...] = _layer_norm(alpha * hx + _dot(o, wo_ref[...]), g_ref[...], b_ref[...])


def _memattn_ln(h, mem_k, mem_v, mem_l, wq, wo, ln_g, ln_b, l, ln_idx, *, alpha, tm):
    b, t, d = h.shape
    mlen, w = mem_k.shape[2:]
    mem_spec = pl.BlockSpec((None, None, mlen, w), lambda bi, ti: (mem_l, bi, 0, 0))
    return pl.pallas_call(
        functools.partial(_memattn_ln_kernel, alpha=alpha, scale=MEM_HEAD_DIM ** -0.5),
        grid=(b, t // tm),
        in_specs=[
            pl.BlockSpec((None, tm, d), lambda bi, ti: (bi, ti, 0)),
            pl.BlockSpec((None, d, w), lambda bi, ti: (l, 0, 0)),
            mem_spec, mem_spec,
            pl.BlockSpec((None, w, d), lambda bi, ti: (l, 0, 0)),
            pl.BlockSpec((None, 1, d), lambda bi, ti: (ln_idx, 0, 0)),
            pl.BlockSpec((None, 1, d), lambda bi, ti: (ln_idx, 0, 0)),
        ],
        out_specs=pl.BlockSpec((None, tm, d), lambda bi, ti: (bi, ti, 0)),
        out_shape=jax.ShapeDtypeStruct((b, t, d), F32),
        compiler_params=_params(("parallel", "parallel")),
        name="memattn_ln",
    )(h, wq, mem_k, mem_v, wo, ln_g, ln_b)


def kernel(x_prompt, x_sample, cache_k, cache_v, cache_mem_k, cache_mem_v, state_pool, state_ssm_re, state_ssm_im, page_table, mem_prompt, ln_g, ln_b, ffn1_gate, ffn1_up, ffn1_down, w_in, b_gate, pool_w, pool_scale, ssm_lam_re, ssm_lam_im, ssm_log_dt, ssm_b_re, ssm_b_im, ssm_c_re, ssm_c_im, ssm_d, ssm_glu_w, ssm_glu_b, w_proj_pool, w_proj_ssm, w_proj_attn, w_out, mem_wq, mem_wk, mem_wv, mem_wo, ffn2_gate, ffn2_up, ffn2_down):
    depth = ln_g.shape[0]
    bp, tp, d = x_prompt.shape
    bs, ts, _ = x_sample.shape
    pool_width = state_pool.shape[3]
    n_groups, n_state = state_ssm_re.shape[2:]
    ssm_width = n_groups * SSM_GROUP
    n_heads = cache_k.shape[3]
    attn_width = n_heads * HEAD_DIM
    n_pages = page_table.shape[1]
    past_len = n_pages * PAGE_SIZE
    n_past_blk = n_pages // PAGES_PER_BLOCK
    mem_len = mem_prompt.shape[1]
    mem_width = N_MEM_HEADS * MEM_HEAD_DIM
    alpha = (2 * depth) ** 0.25
    ssm_col = pool_width
    q_col = ssm_col + ssm_width
    k_col = q_col + attn_width
    v_col = k_col + attn_width
    gate_col = v_col + attn_width

    ln_g2 = ln_g.reshape(depth * 4, 1, d)
    ln_b2 = ln_b.reshape(depth * 4, 1, d)
    b_gate2 = b_gate.reshape(depth, 1, N_BRANCH * d)
    pool_scale2 = pool_scale.reshape(depth, 1, pool_width)
    n_chunk = 4
    lam_re2 = ssm_lam_re.reshape(depth, 1, n_groups * n_state)
    lam_im2 = ssm_lam_im.reshape(depth, 1, n_groups * n_state)
    ldt_rep = jnp.repeat(ssm_log_dt, n_state, axis=1).reshape(depth, 1, n_groups * n_state)
    b_re_c = _block_diag_chunks(jnp.swapaxes(ssm_b_re, 2, 3), n_chunk)
    b_im_c = _block_diag_chunks(jnp.swapaxes(ssm_b_im, 2, 3), n_chunk)
    c_re_c = _block_diag_chunks(jnp.swapaxes(ssm_c_re, 2, 3), n_chunk)
    c_im_c = _block_diag_chunks(jnp.swapaxes(ssm_c_im, 2, 3), n_chunk)
    ssm_d2 = ssm_d.reshape(depth, 1, ssm_width)
    glu_b2 = ssm_glu_b.reshape(depth, 1, ssm_width)
    slopes = jnp.exp2(-8.0 * jnp.arange(1, n_heads + 1, dtype=F32) / n_heads)
    cache_k4 = cache_k.reshape(cache_k.shape[0], cache_k.shape[1], PAGE_SIZE * n_heads, HEAD_DIM)
    cache_v4 = cache_v.reshape(cache_v.shape[0], cache_v.shape[1], PAGE_SIZE * n_heads, HEAD_DIM)
    cmem_k = cache_mem_k.reshape(depth, bs, mem_len, mem_width)
    cmem_v = cache_mem_v.reshape(depth, bs, mem_len, mem_width)
    mem_rows = mem_prompt.reshape(bp * mem_len, d)
    zero_pool = jnp.zeros((bp, POOL_HALO, pool_width), F32)
    zero_state = jnp.zeros((bp, 1, n_groups * n_state), F32)
    state_pool16 = jnp.pad(state_pool, ((0, 0), (0, 0), (POOL_HALO - POOL_BUF, 0), (0, 0)))
    s_re_in = state_ssm_re.reshape(depth, bs, 1, n_groups * n_state)
    s_im_in = state_ssm_im.reshape(depth, bs, 1, n_groups * n_state)

    merge_w = [w.astype(BF16) for w in (w_proj_pool, w_proj_ssm, w_proj_attn, w_out)]

    hp = x_prompt.reshape(bp * tp, d)
    hs = x_sample.reshape(bs * ts, d)
    mp, ms = bp * tp, bs * ts
    outs = {k: [] for k in ("poolp", "rep", "imp", "mkp", "mvp", "ks", "vs", "pools", "res", "ims")}
    kv_stacks = None

    def mix_common(h, z3, a_out, s_out, c_out, l, tm):
        m = h.shape[0]
        return _merge_ln(a_out.reshape(m, -1), s_out.reshape(m, -1), c_out.reshape(m, -1),
                         z3.reshape(m, -1), gate_col, b_gate2, *merge_w,
                         h, ln_g2, ln_b2, l, 4 * l + 1, alpha=alpha, tm=tm, tn=512)

    for l in range(depth):
        mk_p = _matmul(mem_rows, mem_wk, l, tm=bp * mem_len, tn=mem_width)
        mv_p = _matmul(mem_rows, mem_wv, l, tm=bp * mem_len, tn=mem_width)
        hp, hs, kmeans = _ffn_ln(hp, hs, ffn1_gate, ffn1_up, ffn1_down, ln_g2, ln_b2, l, 4 * l, page_table,
                                 cache_k4, n_heads, alpha=alpha, tm=1024, tf=256, pps=8)
        zp, *kv_stacks = _in_proj(hp, w_in, l, k_col, v_col, attn_width, kv_stacks, tm=2048, tn=512)
        zp = zp.reshape(bp, tp, -1)
        a_out = _pool(zp, zero_pool, pool_w, pool_scale2, l, tl=256, start=0, out_dtype=BF16)
        s_out, re_p, im_p = _s5(zp, ssm_col, zero_state, zero_state, lam_re2, lam_im2, ldt_rep,
                                b_re_c, b_im_c, c_re_c, c_im_c, ssm_d2, ssm_glu_w, glu_b2, l, tl=256,
                                out_dtype=BF16)
        c_out = _moba_prompt(zp, slopes, q_col, k_col, v_col, n_heads, hp=4, cb=2)
        hp = mix_common(hp, zp, a_out, s_out, c_out, l, 1024)
        hp = _memattn_ln(hp.reshape(bp, tp, d), mk_p.reshape(1, bp, mem_len, mem_width),
                         mv_p.reshape(1, bp, mem_len, mem_width), 0, mem_wq, mem_wo, ln_g2, ln_b2, l, 4 * l + 2,
                         alpha=alpha, tm=512).reshape(mp, d)
        outs["poolp"].append(zp[:, tp - POOL_BUF:, :pool_width])
        outs["rep"].append(re_p.reshape(bp, n_groups, n_state))
        outs["imp"].append(im_p.reshape(bp, n_groups, n_state))
        outs["mkp"].append(mk_p.reshape(bp, mem_len, N_MEM_HEADS, MEM_HEAD_DIM))
        outs["mvp"].append(mv_p.reshape(bp, mem_len, N_MEM_HEADS, MEM_HEAD_DIM))

        zs = _matmul(hs, w_in, l, tm=ms, tn=512).reshape(bs, ts, -1)
        a_out = _pool(zs, state_pool16[l], pool_w, pool_scale2, l, tl=ts, start=past_len, out_dtype=F32)
        s_out, re_s, im_s = _s5(zs, ssm_col, s_re_in[l], s_im_in[l], lam_re2, lam_im2, ldt_rep,
                                b_re_c, b_im_c, c_re_c, c_im_c, ssm_d2, ssm_glu_w, glu_b2, l, tl=ts,
                                out_dtype=F32)
        c_out = _moba_sample(zs, kmeans, slopes, cache_k4, cache_v4, page_table, l, q_col, k_col, v_col,
                             n_heads, bps=4)
        hs = mix_common(hs, zs, a_out, s_out, c_out, l, ms)
        hs = _memattn_ln(hs.reshape(bs, ts, d), cmem_k, cmem_v, l, mem_wq, mem_wo, ln_g2, ln_b2, l, 4 * l + 2,
                         alpha=alpha, tm=ts).reshape(ms, d)
        hp, hs = _ffn_ln(hp, hs, ffn2_gate, ffn2_up, ffn2_down, ln_g2, ln_b2, l, 4 * l + 3, page_table,
                         alpha=alpha, tm=1024, tf=256)
        outs["ks"].append(zs[:, :, k_col:v_col].reshape(bs, ts, n_heads, HEAD_DIM))
        outs["vs"].append(zs[:, :, v_col:gate_col].reshape(bs, ts, n_heads, HEAD_DIM))
        outs["pools"].append(jnp.concatenate([state_pool[l], zs[:, :, :pool_width]], axis=1)[:, -POOL_BUF:])
        outs["res"].append(re_s.reshape(bs, n_groups, n_state))
        outs["ims"].append(im_s.reshape(bs, n_groups, n_state))

    st = lambda k: jnp.stack(outs[k])
    kp, vp = (s.reshape(depth, bp, tp, n_heads, HEAD_DIM) for s in kv_stacks)
    return (hp.reshape(bp, tp, d), hs.reshape(bs, ts, d),
            kp, vp, st("poolp"), st("rep"), st("imp"), st("mkp"), st("mvp"),
            st("ks"), st("vs"), st("pools"), st("res"), st("ims"))
```

```python
import functools
import math

import jax
import jax.numpy as jnp
from jax import lax
from jax.experimental import pallas as pl
from jax.experimental.pallas import tpu as pltpu

F32 = jnp.float32
BF16 = jnp.bfloat16

POOL_WINDOWS = (2, 4, 8, 16)
POOL_BUF = max(POOL_WINDOWS) - 1
POOL_HALO = 16
SSM_GROUP = 16
SSM_STATE = 64
HEAD_DIM = 128
MOBA_BLOCK = 256
MOBA_TOPK = 3
PAGE_SIZE = 128
PAGES_PER_BLOCK = MOBA_BLOCK // PAGE_SIZE
N_MEM_HEADS = 4
MEM_HEAD_DIM = 128
N_BRANCH = 3
LN_EPS = 1e-5
NEG_INF = -1e30
LANE = 128
SUBLANES = 8
SCAN_COLS = 4
VMEM_LIMIT = 56 * 1024 * 1024


def _params(sem, vmem=VMEM_LIMIT):
    return pltpu.CompilerParams(dimension_semantics=sem, vmem_limit_bytes=vmem)


def _dot(a, b):
    return jnp.dot(a.astype(BF16), b.astype(BF16), preferred_element_type=F32)


def _dot_nt(a, b):
    return lax.dot_general(a.astype(BF16), b.astype(BF16), (((1,), (1,)), ((), ())),
                           preferred_element_type=F32)


def _split(a):
    hi = a.astype(BF16)
    lo = (a - hi.astype(F32)).astype(BF16)
    return hi, lo


def _dot3(a, b):
    ah, al = _split(a)
    bh, bl = _split(b)
    d = functools.partial(jnp.dot, preferred_element_type=F32)
    return d(ah, bh) + (d(ah, bl) + d(al, bh))


def _dot3_nt(a, b):
    ah, al = _split(a)
    bh, bl = _split(b)
    d = functools.partial(lax.dot_general, dimension_numbers=(((1,), (1,)), ((), ())),
                          preferred_element_type=F32)
    return d(ah, bh) + (d(ah, bl) + d(al, bh))


def _layer_norm(y, g, b):
    mu = jnp.mean(y, axis=-1, keepdims=True)
    d = y - mu
    var = jnp.mean(d * d, axis=-1, keepdims=True)
    return d * lax.rsqrt(var + LN_EPS) * g + b


def _sigmoid(x):
    return 0.5 * jnp.tanh(0.5 * x) + 0.5


def _ffn_ln_kernel(pt_ref, x_ref, xs_ref, wg_ref, wu_ref, wd_ref, g_ref, b_ref, *refs,
                   alpha, n_heads, pps, job_steps, steps_per_seq):
    pages = refs[:pps]
    if pps:
        o_ref, os_ref, km_ref, xb_ref, xsb_ref = refs[pps:]
    else:
        o_ref, os_ref, xb_ref, xsb_ref = refs
    i, f = pl.program_id(0), pl.program_id(1)

    if pps:
        step = i * pl.num_programs(1) + f

        @pl.when(step < job_steps)
        def _():
            jj = lax.rem(step, steps_per_seq)
            for bi in range(pps // PAGES_PER_BLOCK):
                tot = jnp.zeros((n_heads, HEAD_DIM), F32)
                for pg in range(PAGES_PER_BLOCK):
                    page = pages[PAGES_PER_BLOCK * bi + pg][...].reshape(PAGE_SIZE, n_heads, HEAD_DIM)
                    tot = tot + jnp.sum(page, axis=0)
                row0 = pl.multiple_of((jj * (pps // PAGES_PER_BLOCK) + bi) * n_heads, n_heads)
                km_ref[pl.ds(row0, n_heads), :] = tot * (1.0 / MOBA_BLOCK)

    def group_step(x_ref, xb_ref, o_ref):
        @pl.when(f == 0)
        def _():
            xb_ref[...] = x_ref[...].astype(BF16)
            o_ref[...] = jnp.zeros_like(o_ref)

        xb = xb_ref[...]
        gate = jnp.dot(xb, wg_ref[...].astype(BF16), preferred_element_type=F32)
        up = jnp.dot(xb, wu_ref[...].astype(BF16), preferred_element_type=F32)
        hid = gate * _sigmoid(gate) * up
        o_ref[...] += _dot(hid, wd_ref[...])

        @pl.when(f == pl.num_programs(1) - 1)
        def _():
            y = alpha * x_ref[...] + 0.5 * o_ref[...]
            o_ref[...] = _layer_norm(y, g_ref[...], b_ref[...])

    group_step(x_ref, xb_ref, o_ref)

    @pl.when(i == 0)
    def _():
        group_step(xs_ref, xsb_ref, os_ref)


def _ffn_ln(x, xs, wg, wu, wd, ln_g, ln_b, l, ln_idx, page_table, cache=None, n_heads=0, *, alpha, tm, tf, pps=0):
    m, d = x.shape
    ms = xs.shape[0]
    f = wg.shape[2]
    grid = (m // tm, f // tf)
    nb, n_pages = page_table.shape
    n_blk = n_pages // PAGES_PER_BLOCK
    steps_per_seq = n_pages // pps if pps else 1
    job_steps = nb * steps_per_seq if pps else 0
    assert job_steps <= grid[0] * grid[1] and (not pps or n_pages % pps == 0)

    def seq_step(i, j):
        s = jnp.minimum(i * grid[1] + j, job_steps - 1)
        return s // steps_per_seq, lax.rem(s, steps_per_seq)

    def page(k):
        return _page_spec(n_heads, lambda i, j, pt: (l, pt[seq_step(i, j)[0], pps * seq_step(i, j)[1] + k], 0, 0))

    in_specs = [
        pl.BlockSpec((tm, d), lambda i, j, pt: (i, 0), pipeline_mode=pl.Buffered(1)),
        pl.BlockSpec((ms, d), lambda i, j, pt: (0, 0), pipeline_mode=pl.Buffered(1)),
        pl.BlockSpec((None, d, tf), lambda i, j, pt: (l, 0, j)),
        pl.BlockSpec((None, d, tf), lambda i, j, pt: (l, 0, j)),
        pl.BlockSpec((None, tf, d), lambda i, j, pt: (l, j, 0)),
        pl.BlockSpec((None, 1, d), lambda i, j, pt: (ln_idx, 0, 0)),
        pl.BlockSpec((None, 1, d), lambda i, j, pt: (ln_idx, 0, 0)),
    ] + [page(k) for k in range(pps)]
    out_specs = [
        pl.BlockSpec((tm, d), lambda i, j, pt: (i, 0), pipeline_mode=pl.Buffered(1)),
        pl.BlockSpec((ms, d), lambda i, j, pt: (0, 0), pipeline_mode=pl.Buffered(1)),
    ]
    out_shape = [jax.ShapeDtypeStruct((m, d), F32), jax.ShapeDtypeStruct((ms, d), F32)]
    if pps:
        out_specs.append(pl.BlockSpec((None, n_blk * n_heads, HEAD_DIM), lambda i, j, pt: (seq_step(i, j)[0], 0, 0)))
        out_shape.append(jax.ShapeDtypeStruct((nb, n_blk * n_heads, HEAD_DIM), F32))
    return pl.pallas_call(
        functools.partial(_ffn_ln_kernel, alpha=alpha, n_heads=n_heads, pps=pps, job_steps=job_steps,
                          steps_per_seq=steps_per_seq),
        grid_spec=pltpu.PrefetchScalarGridSpec(
            num_scalar_prefetch=1, grid=grid, in_specs=in_specs, out_specs=out_specs,
            scratch_shapes=[pltpu.VMEM((tm, d), BF16), pltpu.VMEM((ms, d), BF16)]),
        out_shape=out_shape,
        compiler_params=_params(("arbitrary", "arbitrary")),
        name="ffn_ln",
    )(page_table, x, xs, wg, wu, wd, ln_g, ln_b, *([cache] * pps))


def _matmul_kernel(x_ref, w_ref, o_ref, xb_ref):
    @pl.when(pl.program_id(1) == 0)
    def _():
        xb_ref[...] = x_ref[...].astype(BF16)

    o_ref[...] = jnp.dot(xb_ref[...], w_ref[...].astype(BF16), preferred_element_type=F32)


def _matmul(x, w, l, *, tm, tn):
    m, k = x.shape
    n = w.shape[2]
    return pl.pallas_call(
        _matmul_kernel,
        grid=(m // tm, n // tn),
        in_specs=[
            pl.BlockSpec((tm, k), lambda i, j: (i, 0), pipeline_mode=pl.Buffered(1)),
            pl.BlockSpec((None, k, tn), lambda i, j: (l, 0, j)),
        ],
        out_specs=pl.BlockSpec((tm, tn), lambda i, j: (i, j)),
        out_shape=jax.ShapeDtypeStruct((m, n), F32),
        scratch_shapes=[pltpu.VMEM((tm, k), BF16)],
        compiler_params=_params(("parallel", "arbitrary")),
        name="matmul",
    )(x, w)


def _in_proj_kernel(*refs, k_lo, v_lo, n_kv, aliased):
    x_ref, xs_ref, w_ref = refs[:3]
    z_ref, zs_ref, ko_ref, vo_ref, xb_ref, xsb_ref = refs[5:] if aliased else refs[3:]
    i, j = pl.program_id(0), pl.program_id(1)

    @pl.when(j == 0)
    def _():
        xb_ref[...] = x_ref[...].astype(BF16)

    res = jnp.dot(xb_ref[...], w_ref[...].astype(BF16), preferred_element_type=F32)
    z_ref[...] = res

    @pl.when(i == 0)
    def _():
        @pl.when(j == 0)
        def _():
            xsb_ref[...] = xs_ref[...].astype(BF16)

        zs_ref[...] = jnp.dot(xsb_ref[...], w_ref[...].astype(BF16), preferred_element_type=F32)

    @pl.when((j >= k_lo) & (j < k_lo + n_kv))
    def _():
        ko_ref[...] = res

    @pl.when((j >= v_lo) & (j < v_lo + n_kv))
    def _():
        vo_ref[...] = res


def _in_proj(x, xs, w, l, k_col, v_col, kv_width, stacks, *, tm, tn):
    m, k = x.shape
    ms = xs.shape[0]
    nl, _, n = w.shape
    n_j = n // tn
    assert k_col % tn == 0 and v_col % tn == 0 and kv_width % tn == 0
    k_lo, v_lo, n_kv = k_col // tn, v_col // tn, kv_width // tn
    aliased = stacks is not None

    def kv_spec(lo):
        return pl.BlockSpec((None, tm, tn), lambda i, j: (l, i, jnp.clip(j - lo, 0, n_kv - 1)),
                            pipeline_mode=pl.Buffered(1))

    stack_shape = jax.ShapeDtypeStruct((nl, m, kv_width), F32)
    return pl.pallas_call(
        functools.partial(_in_proj_kernel, k_lo=k_lo, v_lo=v_lo, n_kv=n_kv, aliased=aliased),
        grid=(m // tm, n_j),
        in_specs=[
            pl.BlockSpec((tm, k), lambda i, j: (i, 0), pipeline_mode=pl.Buffered(1)),
            pl.BlockSpec((ms, k), lambda i, j: (0, 0), pipeline_mode=pl.Buffered(1)),
            pl.BlockSpec((None, k, tn), lambda i, j: (l, 0, j)),
        ] + ([pl.BlockSpec(memory_space=pl.ANY)] * 2 if aliased else []),
        out_specs=[
            pl.BlockSpec((tm, tn), lambda i, j: (i, j)),
            pl.BlockSpec((ms, tn), lambda i, j: (0, jnp.where(i == 0, j, n_j - 1))),
            kv_spec(k_lo), kv_spec(v_lo),
        ],
        out_shape=[jax.ShapeDtypeStruct((m, n), F32), jax.ShapeDtypeStruct((ms, n), F32), stack_shape, stack_shape],
        input_output_aliases={3: 2, 4: 3} if aliased else {},
        scratch_shapes=[pltpu.VMEM((tm, k), BF16), pltpu.VMEM((ms, k), BF16)],
        compiler_params=_params(("arbitrary", "arbitrary")),
        name="in_proj",
    )(x, xs, w, *(stacks if aliased else ()))


def _pool_kernel(u_ref, buf_ref, w_ref, scale_ref, o_ref, xx_ref, *, tl, start):
    ti = pl.program_id(1)

    @pl.when(ti == 0)
    def _():
        xx_ref[0:POOL_HALO, :] = buf_ref[...]

    u = u_ref[...]
    xx_ref[POOL_HALO:POOL_HALO + tl, :] = u
    pos = start + ti * tl + lax.broadcasted_iota(jnp.int32, (tl, 1), 0)
    for gi, w in enumerate(POOL_WINDOWS):
        cols = slice(gi * LANE, (gi + 1) * LANE)
        win = xx_ref[POOL_HALO:POOL_HALO + tl, cols]
        for i in range(1, w):
            win = win + xx_ref[POOL_HALO - i:POOL_HALO - i + tl, cols]
        cnt = jnp.minimum(pos + 1, w).astype(F32)
        pooled = win / cnt - u[:, cols]
        o_ref[:, cols] = (_dot3(pooled, w_ref[gi]) * scale_ref[:, cols]).astype(o_ref.dtype)
    xx_ref[0:POOL_HALO, :] = xx_ref[tl:tl + POOL_HALO, :]


def _pool(z, buf16, pool_w, pool_scale, l, *, tl, start, out_dtype):
    b, t, _ = z.shape
    w = buf16.shape[2]
    assert w == len(POOL_WINDOWS) * LANE
    return pl.pallas_call(
        functools.partial(_pool_kernel, tl=tl, start=start),
        grid=(b, t // tl),
        in_specs=[
            pl.BlockSpec((None, tl, w), lambda bi, ti: (bi, ti, 0)),
            pl.BlockSpec((None, POOL_HALO, w), lambda bi, ti: (bi, 0, 0)),
            pl.BlockSpec((None, len(POOL_WINDOWS), LANE, LANE), lambda bi, ti: (l, 0, 0, 0)),
            pl.BlockSpec((None, 1, w), lambda bi, ti: (l, 0, 0)),
        ],
        out_specs=pl.BlockSpec((None, tl, w), lambda bi, ti: (bi, ti, 0)),
        out_shape=jax.ShapeDtypeStruct((b, t, w), out_dtype),
        scratch_shapes=[pltpu.VMEM((POOL_HALO + tl, w), F32)],
        compiler_params=_params(("parallel", "arbitrary")),
        name="pool",
    )(z, buf16, pool_w, pool_scale)


def _s5_kernel(u_ref, s0r_ref, s0i_ref, lr_ref, li_ref, ldt_ref, br_ref, bi_ref, cr_ref, ci_ref,
               d_ref, wg_ref, bg_ref, y_ref, sr_ref, si_ref,
               pr_ref, pi_ref, qr_ref, qi_ref, apr_ref, api_ref, zr_ref, zi_ref, car_ref, cai_ref,
               *, tl, pad, n_chunk):
    ti = pl.program_id(1)
    n_state = lr_ref.shape[1]
    n_pass = int(math.log2(tl))
    uc = u_ref.shape[1] // n_chunk
    sc = n_state // n_chunk

    @pl.when(ti == 0)
    def _():
        dt = jnp.exp(ldt_ref[...])
        lr, li = lr_ref[...], li_ref[...]
        mag = jnp.exp(lr * dt)
        a_re, a_im = mag * jnp.cos(li * dt), mag * jnp.sin(li * dt)
        den = lr * lr + li * li
        zr_ref[...] = ((a_re - 1.0) * lr + a_im * li) / den
        zi_ref[...] = (a_im * lr - (a_re - 1.0) * li) / den
        pw_re, pw_im = a_re, a_im
        for j in range(n_pass):
            apr_ref[j:j + 1, :] = pw_re
            api_ref[j:j + 1, :] = pw_im
            pw_re, pw_im = pw_re * pw_re - pw_im * pw_im, 2.0 * (pw_re * pw_im)
        car_ref[...] = s0r_ref[...]
        cai_ref[...] = s0i_ref[...]
        zeros = jnp.zeros((pad, n_state), F32)
        pr_ref[0:pad, :] = zeros
        pi_ref[0:pad, :] = zeros
        qr_ref[0:pad, :] = zeros
        qi_ref[0:pad, :] = zeros

    u = u_ref[...]
    for c in range(n_chunk):
        uu = u[:, c * uc:(c + 1) * uc]
        ub_re = _dot(uu, br_ref[c])
        ub_im = _dot(uu, bi_ref[c])
        cols = slice(c * sc, (c + 1) * sc)
        z_re, z_im = zr_ref[:, cols], zi_ref[:, cols]
        pr_ref[pad:pad + tl, cols] = z_re * ub_re - z_im * ub_im
        pi_ref[pad:pad + tl, cols] = z_re * ub_im + z_im * ub_re
    a_re, a_im = apr_ref[0:1, :], api_ref[0:1, :]
    c_re, c_im = car_ref[...], cai_ref[...]
    pr_ref[pad:pad + 1, :] += a_re * c_re - a_im * c_im
    pi_ref[pad:pad + 1, :] += a_re * c_im + a_im * c_re

    bufs = ((pr_ref, pi_ref), (qr_ref, qi_ref))
    for j in range(n_pass):
        k = 1 << j
        (s_re, s_im), (d_re, d_im) = bufs[j % 2], bufs[(j + 1) % 2]

        def col_body(cb, carry, k=k, j=j, s_re=s_re, s_im=s_im, d_re=d_re, d_im=d_im):
            for sub in range(SCAN_COLS):
                cols = pl.ds(pl.multiple_of((cb * SCAN_COLS + sub) * LANE, LANE), LANE)
                ar, ai = apr_ref[j:j + 1, cols], api_ref[j:j + 1, cols]
                cur_re, cur_im = s_re[pad:pad + tl, cols], s_im[pad:pad + tl, cols]
                sh_re, sh_im = s_re[pad - k:pad - k + tl, cols], s_im[pad - k:pad - k + tl, cols]
                d_re[pad:pad + tl, cols] = cur_re + (ar * sh_re - ai * sh_im)
                d_im[pad:pad + tl, cols] = cur_im + (ar * sh_im + ai * sh_re)
            return carry

        lax.fori_loop(0, n_state // (SCAN_COLS * LANE), col_body, 0)
    f_re, f_im = bufs[n_pass % 2]

    car_ref[...] = f_re[pad + tl - 1:pad + tl, :]
    cai_ref[...] = f_im[pad + tl - 1:pad + tl, :]
    sr_ref[...] = car_ref[...]
    si_ref[...] = cai_ref[...]

    ys = []
    for c in range(n_chunk):
        cols = slice(c * sc, (c + 1) * sc)
        ys.append(_dot(f_re[pad:pad + tl, cols], cr_ref[c]) - _dot(f_im[pad:pad + tl, cols], ci_ref[c]))
    y = jnp.concatenate(ys, axis=1) + d_ref[...] * u
    y = 0.5 * y * (1.0 + jnp.tanh(math.sqrt(2.0 / math.pi) * (y + 0.044715 * (y * y * y))))
    y_ref[...] = (y * _sigmoid(_dot(y, wg_ref[...]) + bg_ref[...])).astype(y_ref.dtype)


def _s5_seg_kernel(u_ref, s0r_ref, s0i_ref, lr_ref, li_ref, ldt_ref, br_ref, bi_ref, cr_ref, ci_ref,
                   d_ref, wg_ref, bg_ref, y_ref, sr_ref, si_ref,
                   pr_ref, pi_ref, qr_ref, qi_ref, apr_ref, api_ref, zr_ref, zi_ref, car_ref, cai_ref,
                   axr_ref, axi_ref, e0r_ref, e0i_ref, e1r_ref, e1i_ref, pm_ref, pmt_ref,
                   *, tl, pad, n_chunk):
    ti = pl.program_id(1)
    n_state = lr_ref.shape[1]
    seg_len = tl // SUBLANES
    n_local = int(math.log2(seg_len))
    n_link = int(math.log2(SUBLANES))
    uc = u_ref.shape[1] // n_chunk
    sc = n_state // n_chunk

    def cmul(ar, ai, br, bi):
        return ar * br - ai * bi, ar * bi + ai * br

    @pl.when(ti == 0)
    def _():
        dt = jnp.exp(ldt_ref[...])
        lr, li = lr_ref[...], li_ref[...]
        mag = jnp.exp(lr * dt)
        a_re, a_im = mag * jnp.cos(li * dt), mag * jnp.sin(li * dt)
        den = lr * lr + li * li
        zr_ref[...] = ((a_re - 1.0) * lr + a_im * li) / den
        zi_ref[...] = (a_im * lr - (a_re - 1.0) * li) / den
        pw_re, pw_im = a_re, a_im
        for j in range(n_local + n_link):
            apr_ref[j:j + 1, :] = pw_re
            api_ref[j:j + 1, :] = pw_im
            pw_re, pw_im = pw_re * pw_re - pw_im * pw_im, 2.0 * (pw_re * pw_im)
        axr_ref[0:SUBLANES, :] = jnp.broadcast_to(a_re, (SUBLANES, n_state))
        axi_ref[0:SUBLANES, :] = jnp.broadcast_to(a_im, (SUBLANES, n_state))
        for j in range(n_local):
            n_rows = SUBLANES << j
            nr, ni = cmul(axr_ref[0:n_rows, :], axi_ref[0:n_rows, :], apr_ref[j:j + 1, :], api_ref[j:j + 1, :])
            axr_ref[n_rows:2 * n_rows, :] = nr
            axi_ref[n_rows:2 * n_rows, :] = ni
        car_ref[...] = s0r_ref[...]
        cai_ref[...] = s0i_ref[...]
        zeros = jnp.zeros((pad, n_state), F32)
        for ref in (pr_ref, pi_ref, qr_ref, qi_ref):
            ref[0:pad, :] = zeros
        for ref in (e0r_ref, e0i_ref, e1r_ref, e1i_ref):
            ref[0:SUBLANES, :] = jnp.zeros((SUBLANES, n_state), F32)
        i0 = lax.broadcasted_iota(jnp.int32, (tl, tl), 0)
        i1 = lax.broadcasted_iota(jnp.int32, (tl, tl), 1)
        dest = lambda t: jnp.left_shift(jnp.bitwise_and(t, seg_len - 1), n_link) + jnp.right_shift(t, n_local)
        pm_ref[...] = jnp.where(i1 == dest(i0), 1.0, 0.0).astype(BF16)
        pmt_ref[...] = jnp.where(i0 == dest(i1), 1.0, 0.0).astype(BF16)

    u = u_ref[...]
    up = jnp.dot(pmt_ref[...], u.astype(BF16), preferred_element_type=F32).astype(BF16)
    for c in range(n_chunk):
        uu = up[:, c * uc:(c + 1) * uc]
        ub_re = _dot(uu, br_ref[c])
        ub_im = _dot(uu, bi_ref[c])
        cols = slice(c * sc, (c + 1) * sc)
        z_re, z_im = zr_ref[:, cols], zi_ref[:, cols]
        pr_ref[pad:pad + tl, cols] = z_re * ub_re - z_im * ub_im
        pi_ref[pad:pad + tl, cols] = z_re * ub_im + z_im * ub_re
    add_re, add_im = cmul(apr_ref[0:1, :], api_ref[0:1, :], car_ref[...], cai_ref[...])
    pr_ref[pad:pad + 1, :] += add_re
    pi_ref[pad:pad + 1, :] += add_im

    bufs = ((pr_ref, pi_ref), (qr_ref, qi_ref))
    for j in range(n_local):
        k = SUBLANES << j
        (s_re, s_im), (d_re, d_im) = bufs[j % 2], bufs[(j + 1) % 2]

        def col_body(cb, carry, k=k, j=j, s_re=s_re, s_im=s_im, d_re=d_re, d_im=d_im):
            for sub in range(SCAN_COLS):
                cols = pl.ds(pl.multiple_of((cb * SCAN_COLS + sub) * LANE, LANE), LANE)
                mr, mi = cmul(apr_ref[j:j + 1, cols], api_ref[j:j + 1, cols],
                              s_re[pad - k:pad - k + tl, cols], s_im[pad - k:pad - k + tl, cols])
                d_re[pad:pad + tl, cols] = s_re[pad:pad + tl, cols] + mr
                d_im[pad:pad + tl, cols] = s_im[pad:pad + tl, cols] + mi
            return carry

        lax.fori_loop(0, n_state // (SCAN_COLS * LANE), col_body, 0)
    f_re, f_im = bufs[n_local % 2]

    e_bufs = ((e0r_ref, e0i_ref), (e1r_ref, e1i_ref))
    lo, hi = SUBLANES, 2 * SUBLANES
    e0r_ref[lo:hi, :] = f_re[pad + tl - SUBLANES:pad + tl, :]
    e0i_ref[lo:hi, :] = f_im[pad + tl - SUBLANES:pad + tl, :]
    for jj in range(n_link):
        k = 1 << jj
        (s_re, s_im), (d_re, d_im) = e_bufs[jj % 2], e_bufs[(jj + 1) % 2]
        mr, mi = cmul(apr_ref[n_local + jj:n_local + jj + 1, :], api_ref[n_local + jj:n_local + jj + 1, :],
                      s_re[lo - k:hi - k, :], s_im[lo - k:hi - k, :])
        d_re[lo:hi, :] = s_re[lo:hi, :] + mr
        d_im[lo:hi, :] = s_im[lo:hi, :] + mi
    er_ref, ei_ref = e_bufs[n_link % 2]
    car_ref[...] = er_ref[hi - 1:hi, :]
    cai_ref[...] = ei_ref[hi - 1:hi, :]
    sr_ref[...] = car_ref[...]
    si_ref[...] = cai_ref[...]

    def apply_body(cb, carry):
        for sub in range(SCAN_COLS):
            cols = pl.ds(pl.multiple_of((cb * SCAN_COLS + sub) * LANE, LANE), LANE)
            pe_re, pe_im = er_ref[lo - 1:hi - 1, cols], ei_ref[lo - 1:hi - 1, cols]
            for i in range(seg_len):
                rows = slice(i * SUBLANES, (i + 1) * SUBLANES)
                dst = slice(pad + i * SUBLANES, pad + (i + 1) * SUBLANES)
                mr, mi = cmul(axr_ref[rows, cols], axi_ref[rows, cols], pe_re, pe_im)
                f_re[dst, cols] += mr
                f_im[dst, cols] += mi
        return carry

    lax.fori_loop(0, n_state // (SCAN_COLS * LANE), apply_body, 0)

    ys = []
    for c in range(n_chunk):
        cols = slice(c * sc, (c + 1) * sc)
        ys.append(_dot(f_re[pad:pad + tl, cols], cr_ref[c]) - _dot(f_im[pad:pad + tl, cols], ci_ref[c]))
    y_hi, y_lo = _split(jnp.concatenate(ys, axis=1))
    pm = pm_ref[...]
    y = (jnp.dot(pm, y_hi, preferred_element_type=F32) + jnp.dot(pm, y_lo, preferred_element_type=F32)
         + d_ref[...] * u)
    y = 0.5 * y * (1.0 + jnp.tanh(math.sqrt(2.0 / math.pi) * (y + 0.044715 * (y * y * y))))
    y_ref[...] = (y * _sigmoid(_dot(y, wg_ref[...]) + bg_ref[...])).astype(y_ref.dtype)


def _s5(z, col0, s0_re, s0_im, lam_re, lam_im, ldt_rep, b_re_c, b_im_c, c_re_c, c_im_c, ssm_d, glu_w, glu_b, l,
        *, tl, out_dtype):
    b, t, _ = z.shape
    n_state = s0_re.shape[2]
    n_chunk, uc, sc = b_re_c.shape[1:]
    u_w = n_chunk * uc
    assert col0 % u_w == 0 and n_chunk * sc == n_state
    pad = max(SUBLANES, tl // 2)
    n_pass = int(math.log2(tl))
    assert 1 << n_pass == tl
    vec = lambda w: pl.BlockSpec((None, 1, w), lambda bi, ti: (l, 0, 0))
    state_spec = pl.BlockSpec((None, 1, n_state), lambda bi, ti: (bi, 0, 0))
    scratch = ([pltpu.VMEM((pad + tl, n_state), F32)] * 4
               + [pltpu.VMEM((max(SUBLANES, n_pass), n_state), F32)] * 2
               + [pltpu.VMEM((1, n_state), F32)] * 4)
    segmented = tl >= SUBLANES * SUBLANES
    if segmented:
        scratch += ([pltpu.VMEM((tl, n_state), F32)] * 2 + [pltpu.VMEM((2 * SUBLANES, n_state), F32)] * 4
                    + [pltpu.VMEM((tl, tl), BF16)] * 2)
    return pl.pallas_call(
        functools.partial(_s5_seg_kernel if segmented else _s5_kernel, tl=tl, pad=pad, n_chunk=n_chunk),
        grid=(b, t // tl),
        in_specs=[
            pl.BlockSpec((None, tl, u_w), lambda bi, ti: (bi, ti, col0 // u_w)),
            state_spec, state_spec,
            vec(n_state), vec(n_state), vec(n_state),
            pl.BlockSpec((None, n_chunk, uc, sc), lambda bi, ti: (l, 0, 0, 0)),
            pl.BlockSpec((None, n_chunk, uc, sc), lambda bi, ti: (l, 0, 0, 0)),
            pl.BlockSpec((None, n_chunk, sc, uc), lambda bi, ti: (l, 0, 0, 0)),
            pl.BlockSpec((None, n_chunk, sc, uc), lambda bi, ti: (l, 0, 0, 0)),
            vec(u_w),
            pl.BlockSpec((None, u_w, u_w), lambda bi, ti: (l, 0, 0)),
            vec(u_w),
        ],
        out_specs=[
            pl.BlockSpec((None, tl, u_w), lambda bi, ti: (bi, ti, 0)),
            state_spec, state_spec,
        ],
        out_shape=[
            jax.ShapeDtypeStruct((b, t, u_w), out_dtype),
            jax.ShapeDtypeStruct((b, 1, n_state), F32),
            jax.ShapeDtypeStruct((b, 1, n_state), F32),
        ],
        scratch_shapes=scratch,
        compiler_params=_params(("parallel", "arbitrary")),
        name="s5",
    )(z, s0_re, s0_im, lam_re, lam_im, ldt_rep, b_re_c, b_im_c, c_re_c, c_im_c, ssm_d, glu_w, glu_b)


def _block_diag_chunks(w, n_chunk):
    nl, g, r, c = w.shape
    gc = g // n_chunk
    w = w.reshape(nl, n_chunk, gc, r, c)
    eye = jnp.eye(gc, dtype=w.dtype)
    out = w[:, :, :, :, None, :] * eye[None, None, :, None, :, None]
    return out.reshape(nl, n_chunk, gc * r, gc * c)


def _top_blocks(gate, n_valid_col):
    lane = lax.broadcasted_iota(jnp.int32, gate.shape, 1)
    lane_f = lane.astype(F32)
    gate = jnp.where(lane < n_valid_col, gate, NEG_INF)
    sel = jnp.zeros(gate.shape, F32)
    for _ in range(MOBA_TOPK):
        m = jnp.max(gate, axis=1, keepdims=True)
        idx = jnp.min(jnp.where(gate == m, lane_f, float(gate.shape[1])), axis=1, keepdims=True)
        hit = lane_f == idx
        valid = jnp.broadcast_to(m, gate.shape) > 0.5 * NEG_INF
        sel = jnp.where(hit & valid, 1.0, sel)
        gate = jnp.where(hit, NEG_INF, gate)
    return sel


def _pick_column(sel_b, j):
    k = lax.broadcasted_iota(jnp.int32, (LANE, LANE), 0)
    onehot = jnp.where(k == j, 1.0, 0.0).astype(BF16)
    return jnp.dot(sel_b, onehot, preferred_element_type=F32) > 0.5


def _lane_tile(x, k):
    return x if k == 1 else jnp.concatenate([x] * k, axis=1)


def _moba_prompt_kernel(slope_ref, q_ref, k_ref, v_ref, o_ref, kb_ref, vb_ref, km_ref, *, n_blk, hp, cb, scale):
    hg = pl.program_id(1)
    n = pl.program_id(2)
    blk = MOBA_BLOCK
    wc = cb * blk
    rows = hp * blk
    t = k_ref.shape[0]
    stack = lambda per_head: jnp.concatenate(per_head, axis=0)
    head_rows = lambda x, hh: x[hh * blk:(hh + 1) * blk]

    @pl.when(n == 0)
    def _():
        km_ref[...] = jnp.zeros_like(km_ref)
        for hh in range(hp):
            cols = slice(hh * HEAD_DIM, (hh + 1) * HEAD_DIM)
            kb_ref[hh] = k_ref[:, cols].astype(BF16)
            vb_ref[hh, :, 0:HEAD_DIM] = v_ref[:, cols].astype(BF16)
            vb_ref[hh, :, HEAD_DIM:] = jnp.ones((t, HEAD_DIM), BF16)
            for j in range(n_blk):
                km_ref[hh, j:j + 1, :] = jnp.mean(k_ref[j * blk:(j + 1) * blk, cols], axis=0, keepdims=True)

    slopes = [slope_ref[hg * hp + hh] for hh in range(hp)]
    cpos = lax.broadcasted_iota(jnp.int32, (1, wc), 1).astype(F32)
    col_bias = [s * cpos for s in slopes]
    slope_rows = stack([jnp.zeros((blk, LANE), F32) + s for s in slopes])
    qs = [q_ref[:, hh * HEAD_DIM:(hh + 1) * HEAD_DIM] for hh in range(hp)]
    qbs = [(q * scale).astype(BF16) for q in qs]
    gate = stack([_dot_nt(q, km_ref[hh]) for hh, q in enumerate(qs)])
    sel_b = _top_blocks(gate, n).astype(BF16)
    rowmax = lambda x: jnp.broadcast_to(jnp.max(x, axis=1, keepdims=True), (rows, LANE))

    own0 = pl.multiple_of(n * blk, blk)
    causal = stack([lax.broadcasted_iota(jnp.int32, (blk, blk), 0)
                    >= lax.broadcasted_iota(jnp.int32, (blk, blk), 1)] * hp)
    s = stack([_dot_nt(qb, kb_ref[hh, pl.ds(own0, blk), :]) + col_bias[hh][:, :blk] for hh, qb in enumerate(qbs)])
    s = jnp.where(causal, s, NEG_INF)
    m0 = rowmax(s)
    p = jnp.exp(s - _lane_tile(m0, blk // LANE))
    acc0 = stack([_dot(head_rows(p, hh), vb_ref[hh, pl.ds(own0, blk), :]) for hh in range(hp)])

    def body(c, carry):
        m, acc = carry
        c0 = pl.multiple_of(c * wc, wc)
        shift = slope_rows * (n * blk - c * wc).astype(F32)
        s = stack([_dot_nt(qb, kb_ref[hh, pl.ds(c0, wc), :]) + col_bias[hh] for hh, qb in enumerate(qbs)])
        groups = [s[:, g * blk:(g + 1) * blk] for g in range(cb)]
        chosen = [_pick_column(sel_b, c * cb + g) for g in range(cb)]
        m_new = m
        for g in range(cb):
            m_new = jnp.maximum(m_new, jnp.where(chosen[g], rowmax(groups[g]) - shift, NEG_INF))
        p = jnp.concatenate(
            [jnp.exp(groups[g] - _lane_tile(jnp.where(chosen[g], m_new + shift, -NEG_INF), blk // LANE))
             for g in range(cb)], axis=1)
        pv = stack([_dot(head_rows(p, hh), vb_ref[hh, pl.ds(c0, wc), :]) for hh in range(hp)])
        return m_new, _lane_tile(jnp.exp(m - m_new), 2 * HEAD_DIM // LANE) * acc + pv

    _, acc = lax.fori_loop(0, (n + cb - 1) // cb, body, (m0, acc0))
    for hh in range(hp):
        a = head_rows(acc, hh)
        o_ref[:, hh * HEAD_DIM:(hh + 1) * HEAD_DIM] = (a[:, :HEAD_DIM] / a[:, HEAD_DIM:]).astype(o_ref.dtype)


def _moba_prompt(z, slopes, q_col, k_col, v_col, n_heads, *, hp, cb):
    b, t, _ = z.shape
    n_blk = t // MOBA_BLOCK
    w = hp * HEAD_DIM
    assert n_blk * MOBA_BLOCK == t and n_blk <= LANE and n_heads % hp == 0 and n_blk % cb == 0
    assert q_col % w == 0 and k_col % w == 0 and v_col % w == 0
    qc, kc, vc = q_col // w, k_col // w, v_col // w
    whole = lambda c: pl.BlockSpec((None, t, w), lambda bi, hg, n, s: (bi, 0, c + hg),
                                   pipeline_mode=pl.Buffered(1))
    return pl.pallas_call(
        functools.partial(_moba_prompt_kernel, n_blk=n_blk, hp=hp, cb=cb, scale=HEAD_DIM ** -0.5),
        grid_spec=pltpu.PrefetchScalarGridSpec(
            num_scalar_prefetch=1,
            grid=(b, n_heads // hp, n_blk),
            in_specs=[
                pl.BlockSpec((None, MOBA_BLOCK, w), lambda bi, hg, n, s: (bi, n, qc + hg)),
                whole(kc), whole(vc),
            ],
            out_specs=pl.BlockSpec((None, MOBA_BLOCK, w), lambda bi, hg, n, s: (bi, n, hg)),
            scratch_shapes=[pltpu.VMEM((hp, t, HEAD_DIM), BF16), pltpu.VMEM((hp, t, 2 * HEAD_DIM), BF16),
                            pltpu.VMEM((hp, LANE, HEAD_DIM), F32)],
        ),
        out_shape=jax.ShapeDtypeStruct((b, t, n_heads * HEAD_DIM), BF16),
        compiler_params=_params(("parallel", "parallel", "arbitrary")),
        name="moba_prompt",
    )(slopes, z, z, z)


def _page_spec(n_heads, index_map):
    return pl.BlockSpec((None, None, PAGE_SIZE * n_heads, HEAD_DIM), index_map)


def _head_major(page_ref, n_heads):
    return jnp.concatenate([page_ref[pl.ds(h, PAGE_SIZE, stride=n_heads), :] for h in range(n_heads)], axis=1)


def _moba_sample_kernel(pt_ref, slope_ref, q_ref, kn_ref, vn_ref, km_ref, *refs,
                        n_heads, n_blk, bps, past_len, scale):
    npg = PAGES_PER_BLOCK * bps
    k_refs, v_refs = refs[:npg], refs[npg:2 * npg]
    o_ref, qbd_ref, sel_ref, slp_ref, m_ref, l_ref, acc_ref = refs[2 * npg:]
    jj = pl.program_id(1)
    t, width = q_ref.shape
    rows = n_heads * t
    rid = lax.broadcasted_iota(jnp.int32, (rows, 1), 0).astype(F32)
    hq = jnp.floor((rid + 0.5) * (1.0 / t))
    tq = rid - t * hq

    @pl.when(jj == 0)
    def _():
        q = q_ref[...]
        c = lax.broadcasted_iota(jnp.int32, (t, width), 1)
        per_head = [jnp.where((c >= h * HEAD_DIM) & (c < (h + 1) * HEAD_DIM), q, 0.0) for h in range(n_heads)]
        qbd_ref[...] = (jnp.concatenate(per_head, axis=0) * scale).astype(BF16)
        gates = []
        for h in range(n_heads):
            km_h = km_ref[pl.ds(h, n_blk, stride=n_heads), :]
            km_h = jnp.concatenate([km_h, jnp.zeros((LANE - n_blk, HEAD_DIM), F32)], axis=0)
            gates.append(_dot_nt(q[:, h * HEAD_DIM:(h + 1) * HEAD_DIM], km_h))
        sel_ref[...] = _top_blocks(jnp.concatenate(gates, axis=0), n_blk).astype(BF16)
        slp = jnp.zeros((rows, 1), F32)
        for h in range(n_heads):
            slp = jnp.where(hq == h, slope_ref[h], slp)
        slp_ref[...] = slp
        fill = jnp.zeros((LANE - t, width), F32)
        s = _dot_nt(qbd_ref[...], jnp.concatenate([kn_ref[...], fill], axis=0))
        cpos = lax.broadcasted_iota(jnp.int32, (rows, LANE), 1).astype(F32)
        s = jnp.where(cpos <= tq, s - slp * (tq - cpos), NEG_INF)
        m0 = jnp.max(s, axis=1, keepdims=True)
        p = jnp.exp(s - m0)
        m_ref[...] = m0
        l_ref[...] = jnp.sum(p, axis=1, keepdims=True)
        acc_ref[...] = _dot(p, jnp.concatenate([vn_ref[...], fill], axis=0))

    slp = slp_ref[...]
    qbd = qbd_ref[...]
    sel_b = sel_ref[...]
    kmat = jnp.concatenate([_head_major(r, n_heads) for r in k_refs], axis=0).astype(BF16)
    vmat = jnp.concatenate([_head_major(r, n_heads) for r in v_refs], axis=0).astype(BF16)
    kpos = lax.broadcasted_iota(jnp.int32, (rows, bps * MOBA_BLOCK), 1).astype(F32)
    s = _dot_nt(qbd, kmat) - slp * (tq - kpos)
    shift = slp * (past_len - jj * (bps * MOBA_BLOCK)).astype(F32)
    groups = [s[:, g * MOBA_BLOCK:(g + 1) * MOBA_BLOCK] for g in range(bps)]
    chosen = [_pick_column(sel_b, jj * bps + g)[:, 0:1] for g in range(bps)]
    m = m_ref[...]
    m_new = m
    for g in range(bps):
        m_new = jnp.maximum(m_new, jnp.where(chosen[g], jnp.max(groups[g], axis=1, keepdims=True) - shift, NEG_INF))
    p = jnp.concatenate([jnp.exp(groups[g] - jnp.where(chosen[g], m_new + shift, -NEG_INF)) for g in range(bps)],
                        axis=1)
    corr = jnp.exp(m - m_new)
    lsum = corr * l_ref[...] + jnp.sum(p, axis=1, keepdims=True)
    acc = corr * acc_ref[...] + _dot(p, vmat)
    m_ref[...], l_ref[...], acc_ref[...] = m_new, lsum, acc

    @pl.when(jj == pl.num_programs(1) - 1)
    def _():
        out = acc / lsum
        o_ref[...] = jnp.concatenate(
            [out[h * t:(h + 1) * t, h * HEAD_DIM:(h + 1) * HEAD_DIM] for h in range(n_heads)], axis=1)


def _moba_sample(z, kmeans, slopes, cache_k, cache_v, page_table, l, q_col, k_col, v_col, n_heads, *, bps):
    b, t, _ = z.shape
    n_pages = page_table.shape[1]
    n_blk = n_pages // PAGES_PER_BLOCK
    width = n_heads * HEAD_DIM
    npg = PAGES_PER_BLOCK * bps
    rows = n_heads * t
    assert n_blk % bps == 0 and n_blk <= LANE and t <= LANE and rows % 16 == 0
    assert q_col % width == 0 and k_col % width == 0 and v_col % width == 0

    def page(k):
        return _page_spec(n_heads, lambda bi, jj, pt, sl: (l, pt[bi, npg * jj + k], 0, 0))

    tok = lambda col: pl.BlockSpec((None, t, width), lambda bi, jj, pt, sl: (bi, 0, col // width))
    return pl.pallas_call(
        functools.partial(_moba_sample_kernel, n_heads=n_heads, n_blk=n_blk, bps=bps,
                          past_len=n_pages * PAGE_SIZE, scale=HEAD_DIM ** -0.5),
        grid_spec=pltpu.PrefetchScalarGridSpec(
            num_scalar_prefetch=2,
            grid=(b, n_blk // bps),
            in_specs=[tok(q_col), tok(k_col), tok(v_col),
                      pl.BlockSpec((None, n_blk * n_heads, HEAD_DIM), lambda bi, jj, pt, sl: (bi, 0, 0))]
            + [page(k) for k in range(npg)] + [page(k) for k in range(npg)],
            out_specs=pl.BlockSpec((None, t, width), lambda bi, jj, pt, sl: (bi, 0, 0)),
            scratch_shapes=[pltpu.VMEM((rows, width), BF16), pltpu.VMEM((rows, LANE), BF16),
                            pltpu.VMEM((rows, 1), F32), pltpu.VMEM((rows, 1), F32), pltpu.VMEM((rows, 1), F32),
                            pltpu.VMEM((rows, width), F32)],
        ),
        out_shape=jax.ShapeDtypeStruct((b, t, width), F32),
        compiler_params=_params(("parallel", "arbitrary")),
        name="moba_sample",
    )(page_table, slopes, z, z, z, kmeans, *([cache_k] * npg), *([cache_v] * npg))


def _merge_ln_kernel(a_ref, s_ref, c_ref, g0_ref, g1_ref, g2_ref, bg0_ref, bg1_ref, bg2_ref,
                     wp_ref, ws_ref, wa_ref, wo_ref, h_ref, g_ref, b_ref, o_ref, *, alpha):
    j = pl.program_id(1)

    @pl.when(j == 0)
    def _():
        o_ref[...] = jnp.zeros_like(o_ref)

    gated = lambda g_ref, bg_ref, x_ref, w_ref: (
        (jnp.tanh(0.5 * (g_ref[...] + bg_ref[...])) + 1.0) * _dot(x_ref[...], w_ref[...]))
    merged = 0.5 * (gated(g0_ref, bg0_ref, a_ref, wp_ref) + gated(g1_ref, bg1_ref, s_ref, ws_ref)
                    + gated(g2_ref, bg2_ref, c_ref, wa_ref))
    o_ref[...] += _dot(merged, wo_ref[...])

    @pl.when(j == pl.num_programs(1) - 1)
    def _():
        o_ref[...] = _layer_norm(alpha * h_ref[...] + o_ref[...], g_ref[...], b_ref[...])


def _merge_ln(a_out, s_out, c_out, z, gate_col, b_gate, wp, ws, wa, wo, h, ln_g, ln_b, l, ln_idx,
              *, alpha, tm, tn):
    m, d = h.shape
    gc = gate_col // tn
    nj = d // tn
    row = lambda w: pl.BlockSpec((tm, w), lambda i, j: (i, 0), pipeline_mode=pl.Buffered(1))
    gate = lambda g: pl.BlockSpec((tm, tn), lambda i, j: (i, gc + g * nj + j))
    bias = lambda g: pl.BlockSpec((None, 1, tn), lambda i, j: (l, 0, g * nj + j))
    wcol = lambda k: pl.BlockSpec((None, k, tn), lambda i, j: (l, 0, j))
    return pl.pallas_call(
        functools.partial(_merge_ln_kernel, alpha=alpha),
        grid=(m // tm, nj),
        in_specs=[
            row(a_out.shape[1]), row(s_out.shape[1]), row(c_out.shape[1]),
            gate(0), gate(1), gate(2), bias(0), bias(1), bias(2),
            wcol(a_out.shape[1]), wcol(s_out.shape[1]), wcol(c_out.shape[1]),
            pl.BlockSpec((None, tn, d), lambda i, j: (l, j, 0)),
            row(d),
            pl.BlockSpec((None, 1, d), lambda i, j: (ln_idx, 0, 0)),
            pl.BlockSpec((None, 1, d), lambda i, j: (ln_idx, 0, 0)),
        ],
        out_specs=pl.BlockSpec((tm, d), lambda i, j: (i, 0), pipeline_mode=pl.Buffered(1)),
        out_shape=jax.ShapeDtypeStruct((m, d), F32),
        compiler_params=_params(("parallel", "arbitrary")),
        name="merge_ln",
    )(a_out, s_out, c_out, z, z, z, b_gate, b_gate, b_gate, wp, ws, wa, wo, h, ln_g, ln_b)


def _memattn_ln_kernel(h_ref, wq_ref, mk_ref, mv_ref, wo_ref, g_ref, b_ref, o_ref, *, alpha, scale):
    hx = h_ref[...]
    q = _dot(hx, wq_ref[...])
    mk, mv = mk_ref[...], mv_ref[...]
    outs = []
    for hd in range(N_MEM_HEADS):
        cols = slice(hd * MEM_HEAD_DIM, (hd + 1) * MEM_HEAD_DIM)
        s = _dot_nt(q[:, cols], mk[:, cols]) * scale
        s = s - jnp.max(s, axis=1, keepdims=True)
        p = jnp.exp(s)
        p = p / jnp.sum(p, axis=1, keepdims=True)
        outs.append(_dot(p, mv[:, cols]))
    o = jnp.concatenate(outs, axis=1)
    o_ref[...] = _layer_norm(alpha * hx + _dot(o, wo_ref[...]), g_ref[...], b_ref[...])


def _memattn_ln(h, mem_k, mem_v, mem_l, wq, wo, ln_g, ln_b, l, ln_idx, *, alpha, tm):
    b, t, d = h.shape
    mlen, w = mem_k.shape[2:]
    mem_spec = pl.BlockSpec((None, None, mlen, w), lambda bi, ti: (mem_l, bi, 0, 0))
    return pl.pallas_call(
        functools.partial(_memattn_ln_kernel, alpha=alpha, scale=MEM_HEAD_DIM ** -0.5),
        grid=(b, t // tm),
        in_specs=[
            pl.BlockSpec((None, tm, d), lambda bi, ti: (bi, ti, 0)),
            pl.BlockSpec((None, d, w), lambda bi, ti: (l, 0, 0)),
            mem_spec, mem_spec,
            pl.BlockSpec((None, w, d), lambda bi, ti: (l, 0, 0)),
            pl.BlockSpec((None, 1, d), lambda bi, ti: (ln_idx, 0, 0)),
            pl.BlockSpec((None, 1, d), lambda bi, ti: (ln_idx, 0, 0)),
        ],
        out_specs=pl.BlockSpec((None, tm, d), lambda bi, ti: (bi, ti, 0)),
        out_shape=jax.ShapeDtypeStruct((b, t, d), F32),
        compiler_params=_params(("parallel", "parallel")),
        name="memattn_ln",
    )(h, wq, mem_k, mem_v, wo, ln_g, ln_b)


def kernel(x_prompt, x_sample, cache_k, cache_v, cache_mem_k, cache_mem_v, state_pool, state_ssm_re, state_ssm_im, page_table, mem_prompt, ln_g, ln_b, ffn1_gate, ffn1_up, ffn1_down, w_in, b_gate, pool_w, pool_scale, ssm_lam_re, ssm_lam_im, ssm_log_dt, ssm_b_re, ssm_b_im, ssm_c_re, ssm_c_im, ssm_d, ssm_glu_w, ssm_glu_b, w_proj_pool, w_proj_ssm, w_proj_attn, w_out, mem_wq, mem_wk, mem_wv, mem_wo, ffn2_gate, ffn2_up, ffn2_down):
    depth = ln_g.shape[0]
    bp, tp, d = x_prompt.shape
    bs, ts, _ = x_sample.shape
    pool_width = state_pool.shape[3]
    n_groups, n_state = state_ssm_re.shape[2:]
    ssm_width = n_groups * SSM_GROUP
    n_heads = cache_k.shape[3]
    attn_width = n_heads * HEAD_DIM
    n_pages = page_table.shape[1]
    past_len = n_pages * PAGE_SIZE
    n_past_blk = n_pages // PAGES_PER_BLOCK
    mem_len = mem_prompt.shape[1]
    mem_width = N_MEM_HEADS * MEM_HEAD_DIM
    alpha = (2 * depth) ** 0.25
    ssm_col = pool_width
    q_col = ssm_col + ssm_width
    k_col = q_col + attn_width
    v_col = k_col + attn_width
    gate_col = v_col + attn_width

    ln_g2 = ln_g.reshape(depth * 4, 1, d)
    ln_b2 = ln_b.reshape(depth * 4, 1, d)
    b_gate2 = b_gate.reshape(depth, 1, N_BRANCH * d)
    pool_scale2 = pool_scale.reshape(depth, 1, pool_width)
    n_chunk = 4
    lam_re2 = ssm_lam_re.reshape(depth, 1, n_groups * n_state)
    lam_im2 = ssm_lam_im.reshape(depth, 1, n_groups * n_state)
    ldt_rep = jnp.repeat(ssm_log_dt, n_state, axis=1).reshape(depth, 1, n_groups * n_state)
    b_re_c = _block_diag_chunks(jnp.swapaxes(ssm_b_re, 2, 3), n_chunk)
    b_im_c = _block_diag_chunks(jnp.swapaxes(ssm_b_im, 2, 3), n_chunk)
    c_re_c = _block_diag_chunks(jnp.swapaxes(ssm_c_re, 2, 3), n_chunk)
    c_im_c = _block_diag_chunks(jnp.swapaxes(ssm_c_im, 2, 3), n_chunk)
    ssm_d2 = ssm_d.reshape(depth, 1, ssm_width)
    glu_b2 = ssm_glu_b.reshape(depth, 1, ssm_width)
    slopes = jnp.exp2(-8.0 * jnp.arange(1, n_heads + 1, dtype=F32) / n_heads)
    cache_k4 = cache_k.reshape(cache_k.shape[0], cache_k.shape[1], PAGE_SIZE * n_heads, HEAD_DIM)
    cache_v4 = cache_v.reshape(cache_v.shape[0], cache_v.shape[1], PAGE_SIZE * n_heads, HEAD_DIM)
    cmem_k = cache_mem_k.reshape(depth, bs, mem_len, mem_width)
    cmem_v = cache_mem_v.reshape(depth, bs, mem_len, mem_width)
    mem_rows = mem_prompt.reshape(bp * mem_len, d)
    zero_pool = jnp.zeros((bp, POOL_HALO, pool_width), F32)
    zero_state = jnp.zeros((bp, 1, n_groups * n_state), F32)
    state_pool16 = jnp.pad(state_pool, ((0, 0), (0, 0), (POOL_HALO - POOL_BUF, 0), (0, 0)))
    s_re_in = state_ssm_re.reshape(depth, bs, 1, n_groups * n_state)
    s_im_in = state_ssm_im.reshape(depth, bs, 1, n_groups * n_state)

    merge_w = [w.astype(BF16) for w in (w_proj_pool, w_proj_ssm, w_proj_attn, w_out)]

    hp = x_prompt.reshape(bp * tp, d)
    hs = x_sample.reshape(bs * ts, d)
    mp, ms = bp * tp, bs * ts
    outs = {k: [] for k in ("poolp", "rep", "imp", "mkp", "mvp", "ks", "vs", "pools", "res", "ims")}
    kv_stacks = None

    def mix_common(h, z3, a_out, s_out, c_out, l, tm):
        m = h.shape[0]
        return _merge_ln(a_out.reshape(m, -1), s_out.reshape(m, -1), c_out.reshape(m, -1),
                         z3.reshape(m, -1), gate_col, b_gate2, *merge_w,
                         h, ln_g2, ln_b2, l, 4 * l + 1, alpha=alpha, tm=tm, tn=512)

    for l in range(depth):
        mk_p = _matmul(mem_rows, mem_wk, l, tm=bp * mem_len, tn=mem_width)
        mv_p = _matmul(mem_rows, mem_wv, l, tm=bp * mem_len, tn=mem_width)
        hp, hs, kmeans = _ffn_ln(hp, hs, ffn1_gate, ffn1_up, ffn1_down, ln_g2, ln_b2, l, 4 * l, page_table,
                                 cache_k4, n_heads, alpha=alpha, tm=1024, tf=256, pps=8)
        zp, zs, *kv_stacks = _in_proj(hp, hs, w_in, l, k_col, v_col, attn_width, kv_stacks, tm=2048, tn=512)
        zp, zs = zp.reshape(bp, tp, -1), zs.reshape(bs, ts, -1)
        a_out = _pool(zp, zero_pool, pool_w, pool_scale2, l, tl=256, start=0, out_dtype=BF16)
        s_out, re_p, im_p = _s5(zp, ssm_col, zero_state, zero_state, lam_re2, lam_im2, ldt_rep,
                                b_re_c, b_im_c, c_re_c, c_im_c, ssm_d2, ssm_glu_w, glu_b2, l, tl=256,
                                out_dtype=BF16)
        c_out = _moba_prompt(zp, slopes, q_col, k_col, v_col, n_heads, hp=4, cb=2)
        hp = mix_common(hp, zp, a_out, s_out, c_out, l, 1024)
        hp = _memattn_ln(hp.reshape(bp, tp, d), mk_p.reshape(1, bp, mem_len, mem_width),
                         mv_p.reshape(1, bp, mem_len, mem_width), 0, mem_wq, mem_wo, ln_g2, ln_b2, l, 4 * l + 2,
                         alpha=alpha, tm=512).reshape(mp, d)
        outs["poolp"].append(zp[:, tp - POOL_BUF:, :pool_width])
        outs["rep"].append(re_p.reshape(bp, n_groups, n_state))
        outs["imp"].append(im_p.reshape(bp, n_groups, n_state))
        outs["mkp"].append(mk_p.reshape(bp, mem_len, N_MEM_HEADS, MEM_HEAD_DIM))
        outs["mvp"].append(mv_p.reshape(bp, mem_len, N_MEM_HEADS, MEM_HEAD_DIM))

        a_out = _pool(zs, state_pool16[l], pool_w, pool_scale2, l, tl=ts, start=past_len, out_dtype=F32)
        s_out, re_s, im_s = _s5(zs, ssm_col, s_re_in[l], s_im_in[l], lam_re2, lam_im2, ldt_rep,
                                b_re_c, b_im_c, c_re_c, c_im_c, ssm_d2, ssm_glu_w, glu_b2, l, tl=ts,
                                out_dtype=F32)
        c_out = _moba_sample(zs, kmeans, slopes, cache_k4, cache_v4, page_table, l, q_col, k_col, v_col,
                             n_heads, bps=4)
        hs = mix_common(hs, zs, a_out, s_out, c_out, l, ms)
        hs = _memattn_ln(hs.reshape(bs, ts, d), cmem_k, cmem_v, l, mem_wq, mem_wo, ln_g2, ln_b2, l, 4 * l + 2,
                         alpha=alpha, tm=ts).reshape(ms, d)
        hp, hs = _ffn_ln(hp, hs, ffn2_gate, ffn2_up, ffn2_down, ln_g2, ln_b2, l, 4 * l + 3, page_table,
                         alpha=alpha, tm=1024, tf=256)
        outs["ks"].append(zs[:, :, k_col:v_col].reshape(bs, ts, n_heads, HEAD_DIM))
        outs["vs"].append(zs[:, :, v_col:gate_col].reshape(bs, ts, n_heads, HEAD_DIM))
        outs["pools"].append(jnp.concatenate([state_pool[l], zs[:, :, :pool_width]], axis=1)[:, -POOL_BUF:])
        outs["res"].append(re_s.reshape(bs, n_groups, n_state))
        outs["ims"].append(im_s.reshape(bs, n_groups, n_state))

    st = lambda k: jnp.stack(outs[k])
    kp, vp = (s.reshape(depth, bp, tp, n_heads, HEAD_DIM) for s in kv_stacks)
    return (hp.reshape(bp, tp, d), hs.reshape(bs, ts, d),
            kp, vp, st("poolp"), st("rep"), st("imp"), st("mkp"), st("mvp"),
            st("ks"), st("vs"), st("pools"), st("res"), st("ims"))
```
